```python
import math
import jax, jax.numpy as jnp
from jax import lax
import numpy as np

D_MODEL = 1024
BATCH = 4
SEQ = 4096
DEPTH = 1
DEC_BATCH = 128
DEC_SEQ = 8
PAST_LEN = 16384
PAGE_SIZE = 128

NSA_HEADS = 8
NSA_GROUPS = 2
NSA_HPG = NSA_HEADS // NSA_GROUPS
NSA_HEAD_DIM = 64
CMP_BLOCK = 32
SEL_BLOCK = 64
TOP_N = 16
WINDOW = 512
MLA_HEADS = 8
MLA_Q_RANK = 384
MLA_KV_RANK = 256
MLA_NOPE_DIM = 64
MLA_ROPE_DIM = 32
MLA_QK_DIM = MLA_NOPE_DIM + MLA_ROPE_DIM
MLA_V_DIM = 64
ROPE_BASE = 10000.0
MEM_LEN = 256
MEM_HEADS = 4
MEM_HEAD_DIM = 128
D_FF = 2816
CONV_W = 3
REL_BUCKETS = 32
REL_MAX_DIST = 128

Q_BLOCK = 128
NORM_EPS = 1e-6
NEG_INF = -1e30
FORCED_SCORE = 1e9
IN_SIZES = (NSA_HEADS * NSA_HEAD_DIM, 6 * NSA_GROUPS * NSA_HEAD_DIM, 3 * NSA_HEADS,
            MLA_Q_RANK, MLA_KV_RANK, MLA_ROPE_DIM, 2 * D_MODEL)
IN_COLS = (NSA_HEADS * NSA_HEAD_DIM + 6 * NSA_GROUPS * NSA_HEAD_DIM + 3 * NSA_HEADS
           + MLA_Q_RANK + MLA_KV_RANK + MLA_ROPE_DIM + 2 * D_MODEL)

kernel_name = 'nsa_mla_gated_hybrid_step'


def rmsnorm(x, g):
    xf = x.astype(jnp.float32)
    y = xf * lax.rsqrt(jnp.mean(xf * xf, axis=-1, keepdims=True) + NORM_EPS)
    return (y * g.astype(jnp.float32)).astype(x.dtype)


def masked_softmax(s, mask):
    return jax.nn.softmax(jnp.where(mask, s, NEG_INF), axis=-1)


def rel_bucket(dist):
    d = jnp.maximum(dist, 0)
    exact = REL_BUCKETS // 2
    logd = jnp.log(jnp.maximum(d, 1).astype(jnp.float32) / exact)
    large = exact + (logd / math.log(REL_MAX_DIST / exact) * (REL_BUCKETS - exact)).astype(jnp.int32)
    return jnp.where(d < exact, d, jnp.minimum(large, REL_BUCKETS - 1))


def apply_rope(x, pos):
    half = x.shape[-1] // 2
    inv = jnp.power(ROPE_BASE, -jnp.arange(half, dtype=jnp.float32) / half)
    ang = pos.astype(jnp.float32)[:, None] * inv[None, :]
    cos = jnp.cos(ang)[:, None, :]
    sin = jnp.sin(ang)[:, None, :]
    xf = x.astype(jnp.float32)
    x1, x2 = xf[..., :half], xf[..., half:]
    return jnp.concatenate([x1 * cos - x2 * sin, x2 * cos + x1 * sin], axis=-1).astype(x.dtype)


def mla_qk(x, g, pos):
    x = rmsnorm(x, g)
    return jnp.concatenate([x[..., :MLA_NOPE_DIM], apply_rope(x[..., MLA_NOPE_DIM:], pos)], axis=-1)


def mla_keys(rows, pos, w_ukv, k_norm):
    c, kr = rows[..., :MLA_KV_RANK], rows[..., MLA_KV_RANK:]
    kv = jnp.einsum('...lc,chd->...lhd', c, w_ukv.reshape(MLA_KV_RANK, MLA_HEADS, MLA_NOPE_DIM + MLA_V_DIM))
    k_nope, v = kv[..., :MLA_NOPE_DIM], kv[..., MLA_NOPE_DIM:]
    kr = jnp.broadcast_to(kr[..., None, :], k_nope.shape[:-1] + (MLA_ROPE_DIM,))
    return mla_qk(jnp.concatenate([k_nope, kr], axis=-1), k_norm, pos), v


def mla_attend(q, q_pos, k, v, k_pos):
    s = jnp.einsum('thd,lhd->thl', q, k, preferred_element_type=jnp.float32) * (MLA_QK_DIM ** -0.5)
    p = masked_softmax(s, k_pos[None, None, :] <= q_pos[:, None, None])
    o = jnp.einsum('thl,lhd->thd', p.astype(v.dtype), v)
    return o.reshape(q.shape[0], MLA_HEADS * MLA_V_DIM)


def nsa_compress(rows_k, rows_v, w1, b1, w2, k_norm):
    def phi(r, i):
        lead = r.shape[:-3]
        nb = r.shape[-3] // CMP_BLOCK
        blk = r.reshape(lead + (nb, CMP_BLOCK, NSA_GROUPS, NSA_HEAD_DIM))
        blk = jnp.swapaxes(blk, -3, -2).reshape(lead + (nb, NSA_GROUPS, CMP_BLOCK * NSA_HEAD_DIM))
        return jax.nn.silu(blk @ w1[i] + b1[i]) @ w2[i]
    return rmsnorm(phi(rows_k, 0), k_norm), phi(rows_v, 1)


def nsa_attend(q, q_pos, gates, kc, vc, ks, vs, kw, vw, kw_pos, rel_bias):
    tq = q.shape[0]
    scale = NSA_HEAD_DIM ** -0.5
    qg = q.reshape(tq, NSA_GROUPS, NSA_HPG, NSA_HEAD_DIM)
    rb = rel_bias.reshape(REL_BUCKETS, NSA_GROUPS, NSA_HPG)
    g_idx = jnp.arange(NSA_GROUPS)
    nc = kc.shape[0]
    blk_end = jnp.arange(nc, dtype=jnp.int32) * CMP_BLOCK + (CMP_BLOCK - 1)
    dist_c = q_pos[:, None] - blk_end[None, :]
    bias_c = jnp.transpose(rb[rel_bucket(dist_c)], (0, 2, 3, 1))
    s_c = jnp.einsum('tghd,ngd->tghn', qg, kc, preferred_element_type=jnp.float32) * scale + bias_c
    valid_c = (dist_c >= 0)[:, None, None, :]
    p_c = jnp.where(valid_c, masked_softmax(s_c, valid_c), 0.0)
    o_c = jnp.einsum('tghn,ngd->tghd', p_c.astype(vc.dtype), vc)
    ns = nc * CMP_BLOCK // SEL_BLOCK
    imp = p_c.sum(axis=2).reshape(tq, NSA_GROUPS, ns, SEL_BLOCK // CMP_BLOCK).sum(-1)
    blk = jnp.arange(ns, dtype=jnp.int32)[None, :]
    cur = (q_pos // SEL_BLOCK)[:, None]
    forced = (blk == 0) | (blk == cur) | (blk == cur - 1)
    imp = jnp.where(forced[:, None], FORCED_SCORE, jnp.where((blk <= cur)[:, None], imp, -1.0))
    _, sel = lax.top_k(imp, min(TOP_N, ns))
    kpos = (sel[..., None] * SEL_BLOCK + jnp.arange(SEL_BLOCK, dtype=jnp.int32)).reshape(tq, NSA_GROUPS, -1)
    ksg = ks[kpos, g_idx[None, :, None]]
    vsg = vs[kpos, g_idx[None, :, None]]
    dist_s = q_pos[:, None, None] - kpos
    bias_s = jnp.transpose(rb[rel_bucket(dist_s), g_idx[None, :, None]], (0, 1, 3, 2))
    s_s = jnp.einsum('tghd,tgkd->tghk', qg, ksg, preferred_element_type=jnp.float32) * scale + bias_s
    p_s = masked_softmax(s_s, (dist_s >= 0)[:, :, None, :])
    o_s = jnp.einsum('tghk,tgkd->tghd', p_s.astype(vsg.dtype), vsg)
    dist_w = q_pos[:, None] - kw_pos[None, :]
    valid_w = (dist_w >= 0) & (dist_w < WINDOW) & (kw_pos >= 0)[None, :]
    bias_w = jnp.transpose(rb[rel_bucket(dist_w)], (0, 2, 3, 1))
    s_w = jnp.einsum('tghd,lgd->tghl', qg, kw, preferred_element_type=jnp.float32) * scale + bias_w
    p_w = masked_softmax(s_w, valid_w[:, None, None, :])
    o_w = jnp.einsum('tghl,lgd->tghd', p_w.astype(vw.dtype), vw)
    g = gates.reshape(tq, NSA_GROUPS, NSA_HPG, 3)
    o = g[..., 0:1] * o_c + g[..., 1:2] * o_s + g[..., 2:3] * o_w
    return o.reshape(tq, NSA_HEADS * NSA_HEAD_DIM)


def project_mixer(h, pos, w_in, nsa_q_norm, nsa_k_norm, mla_q_norm, mla_kv_norm, mla_w_uq, mla_qk_q_norm):
    b, t = h.shape[:2]
    offs = tuple(int(v) for v in np.cumsum(IN_SIZES)[:-1])
    q_n, kv_n, g_n, cq, ckv, kr, mg = jnp.split(h @ w_in, offs, axis=-1)
    q_n = rmsnorm(q_n.reshape(b, t, NSA_HEADS, NSA_HEAD_DIM), nsa_q_norm)
    kv_n = kv_n.reshape(b, t, 6, NSA_GROUPS, NSA_HEAD_DIM)
    k_sel = rmsnorm(kv_n[:, :, 2], nsa_k_norm[1])
    k_win = rmsnorm(kv_n[:, :, 4], nsa_k_norm[2])
    nsa_rows = jnp.stack([kv_n[:, :, 0], kv_n[:, :, 1], k_sel, kv_n[:, :, 3]], axis=2)
    win_rows = jnp.stack([k_win, kv_n[:, :, 5]], axis=2)
    gates = jax.nn.sigmoid(g_n.reshape(b, t, NSA_HEADS, 3))
    cq = rmsnorm(cq, mla_q_norm)
    qm = jnp.einsum('btc,chd->bthd', cq, mla_w_uq.reshape(MLA_Q_RANK, MLA_HEADS, MLA_QK_DIM))
    qm = mla_qk(qm, mla_qk_q_norm, pos)
    mla_rows = jnp.concatenate([rmsnorm(ckv, mla_kv_norm), kr], axis=-1)
    merge_g = jax.nn.sigmoid(mg.reshape(b, t, 2, D_MODEL))
    return q_n, gates, nsa_rows, win_rows, qm, mla_rows, merge_g


def merge_out(merge_g, o_n, o_m, nsa_w_o, mla_w_o, w_out):
    return (merge_g[:, :, 0] * (o_n @ nsa_w_o) + merge_g[:, :, 1] * (o_m @ mla_w_o)) @ w_out


def prompt_attention(q_n, gates, nsa_rows, win_rows, qm, mla_rows, w1, b1, w2, kc_norm,
                     mla_w_ukv, mla_k_norm, rel_bias):
    b, t = q_n.shape[:2]
    pos = jnp.arange(t, dtype=jnp.int32)
    kc, vc = nsa_compress(nsa_rows[:, :, 0], nsa_rows[:, :, 1], w1, b1, w2, kc_norm)
    ks, vs = nsa_rows[:, :, 2], nsa_rows[:, :, 3]
    kw_pad = jnp.pad(win_rows, ((0, 0), (WINDOW, 0), (0, 0), (0, 0), (0, 0)))
    k_m, v_m = mla_keys(mla_rows, pos, mla_w_ukv, mla_k_norm)
    nqb = t // Q_BLOCK

    def block(i):
        bi = i // nqb
        s0 = (i % nqb) * Q_BLOCK
        qp = s0 + jnp.arange(Q_BLOCK, dtype=jnp.int32)
        sl = lambda a: lax.dynamic_slice_in_dim(a[bi], s0, Q_BLOCK, 0)
        wk = lax.dynamic_slice_in_dim(kw_pad[bi], s0, WINDOW + Q_BLOCK, 0)
        kw_pos = s0 - WINDOW + jnp.arange(WINDOW + Q_BLOCK, dtype=jnp.int32)
        o_n = nsa_attend(sl(q_n), qp, sl(gates), kc[bi], vc[bi], ks[bi], vs[bi],
                         wk[:, 0], wk[:, 1], kw_pos, rel_bias)
        o_m = mla_attend(sl(qm), qp, k_m[bi], v_m[bi], pos)
        return o_n, o_m

    o_n, o_m = lax.map(block, jnp.arange(b * nqb, dtype=jnp.int32))
    return o_n.reshape(b, t, -1), o_m.reshape(b, t, -1)


def sample_attention(q_n, gates, nsa_rows, win_rows, qm, mla_rows, cache_nsa, cache_mla, win_state,
                     page_table, layer, w1, b1, w2, kc_norm, mla_w_ukv, mla_k_norm, rel_bias):
    t = q_n.shape[1]
    q_pos = PAST_LEN + jnp.arange(t, dtype=jnp.int32)
    lk = PAST_LEN + t
    lk_pad = -(-lk // SEL_BLOCK) * SEL_BLOCK
    k_pos = jnp.arange(lk, dtype=jnp.int32)
    w_s = win_state.shape[1]
    wkeys = jnp.concatenate([win_state.astype(win_rows.dtype), win_rows], axis=1)
    kw_pos = PAST_LEN - w_s + jnp.arange(w_s + t, dtype=jnp.int32)

    def seq(args):
        pt, q, g, nr, qmi, mr, wk = args
        past_n = cache_nsa[layer, pt].reshape(PAST_LEN, 4, NSA_GROUPS, NSA_HEAD_DIM)
        full_n = jnp.concatenate([past_n.astype(nr.dtype), nr], axis=0)
        full_n = jnp.pad(full_n, ((0, lk_pad - lk), (0, 0), (0, 0), (0, 0)))
        kc, vc = nsa_compress(full_n[:, 0], full_n[:, 1], w1, b1, w2, kc_norm)
        o_n = nsa_attend(q, q_pos, g, kc, vc, full_n[:, 2], full_n[:, 3], wk[:, 0], wk[:, 1], kw_pos, rel_bias)
        past_m = cache_mla[layer, pt].reshape(PAST_LEN, MLA_KV_RANK + MLA_ROPE_DIM)
        full_m = jnp.concatenate([past_m.astype(mr.dtype), mr], axis=0)
        k, v = mla_keys(full_m, k_pos, mla_w_ukv, mla_k_norm)
        o_m = mla_attend(qmi, q_pos, k, v, k_pos)
        return o_n, o_m

    o_n, o_m = lax.map(seq, (page_table, q_n, gates, nsa_rows, qm, mla_rows, wkeys))
    return o_n, o_m, wkeys[:, -w_s:]


def mem_kv_rows(mem, mem_norm_g, w_kv, k_norm):
    b, m = mem.shape[:2]
    kv = (rmsnorm(mem, mem_norm_g) @ w_kv).reshape(b, m, 2, MEM_HEADS, MEM_HEAD_DIM)
    return jnp.stack([rmsnorm(kv[:, :, 0], k_norm), kv[:, :, 1]], axis=2)


def mem_attend(h, kv, w_q, q_norm, w_o):
    b, t = h.shape[:2]
    q = rmsnorm((h @ w_q).reshape(b, t, MEM_HEADS, MEM_HEAD_DIM), q_norm)
    kv = kv.astype(q.dtype)
    s = jnp.einsum('bthd,bmhd->bhtm', q, kv[:, :, 0], preferred_element_type=jnp.float32) * (MEM_HEAD_DIM ** -0.5)
    p = jax.nn.softmax(s, axis=-1)
    o = jnp.einsum('bhtm,bmhd->bthd', p.astype(q.dtype), kv[:, :, 1])
    return o.reshape(b, t, MEM_HEADS * MEM_HEAD_DIM) @ w_o


def conv_ffn(h, past, w_up, conv_w, conv_b, w_down):
    t = h.shape[1]
    a, v = jnp.split(h @ w_up, 2, axis=-1)
    a_full = jnp.concatenate([past.astype(a.dtype), a], axis=1)
    c = conv_b + conv_w[0] * a_full[:, 0:t]
    for j in range(1, CONV_W):
        c = c + conv_w[j] * a_full[:, j:j + t]
    y = (jax.nn.silu(c) * v) @ w_down
    return y, a_full[:, t:]


def setup_inputs(seed: int = 0) -> dict:
    key = jax.random.key(seed)
    keys = iter(jax.random.split(key, 48))
    f32 = jnp.float32
    nrm = lambda shape, scale: jax.random.normal(next(keys), shape, f32) * scale
    gain = lambda shape: 1.0 + nrm(shape, 0.02)
    n_pages = PAST_LEN // PAGE_SIZE
    n_used = DEC_BATCH * n_pages
    n_pool = (5 * n_used + 3) // 4
    w_s = min(WINDOW, PAST_LEN)
    page_table = jax.random.permutation(next(keys), n_pool)[:n_used].reshape(DEC_BATCH, n_pages).astype(jnp.int32)
    return {
        'x_prompt': nrm((BATCH, SEQ, D_MODEL), 1.0),
        'x_sample': nrm((DEC_BATCH, DEC_SEQ, D_MODEL), 1.0),
        'cache_nsa': nrm((DEPTH, n_pool, PAGE_SIZE, 4, NSA_GROUPS, NSA_HEAD_DIM), 1.0),
        'cache_mla': nrm((DEPTH, n_pool, PAGE_SIZE, MLA_KV_RANK + MLA_ROPE_DIM), 1.0),
        'state_win': nrm((DEPTH, DEC_BATCH, w_s, 2, NSA_GROUPS, NSA_HEAD_DIM), 1.0),
        'cache_mem': nrm((DEPTH, DEC_BATCH, MEM_LEN, 2, MEM_HEADS, MEM_HEAD_DIM), 1.0),
        'state_conv': nrm((DEPTH, DEC_BATCH, CONV_W - 1, D_FF), 1.0),
        'page_table': page_table,
        'mem_prompt': nrm((BATCH, MEM_LEN, D_MODEL), 1.0),
        'rel_bias': nrm((REL_BUCKETS, NSA_HEADS), 0.5),
        'norm1_g': gain((DEPTH, D_MODEL)),
        'w_in': nrm((DEPTH, D_MODEL, IN_COLS), D_MODEL ** -0.5),
        'nsa_q_norm': gain((DEPTH, NSA_HEAD_DIM)),
        'nsa_k_norm': gain((DEPTH, 3, NSA_HEAD_DIM)),
        'nsa_cmp_w1': nrm((DEPTH, 2, CMP_BLOCK * NSA_HEAD_DIM, NSA_HEAD_DIM), (CMP_BLOCK * NSA_HEAD_DIM) ** -0.5),
        'nsa_cmp_b1': nrm((DEPTH, 2, NSA_HEAD_DIM), 0.02),
        'nsa_cmp_w2': nrm((DEPTH, 2, NSA_HEAD_DIM, NSA_HEAD_DIM), NSA_HEAD_DIM ** -0.5),
        'nsa_w_o': nrm((DEPTH, NSA_HEADS * NSA_HEAD_DIM, D_MODEL), (NSA_HEADS * NSA_HEAD_DIM) ** -0.5),
        'mla_q_norm': gain((DEPTH, MLA_Q_RANK)),
        'mla_kv_norm': gain((DEPTH, MLA_KV_RANK)),
        'mla_w_uq': nrm((DEPTH, MLA_Q_RANK, MLA_HEADS * MLA_QK_DIM), MLA_Q_RANK ** -0.5),
        'mla_w_ukv': nrm((DEPTH, MLA_KV_RANK, MLA_HEADS * (MLA_NOPE_DIM + MLA_V_DIM)), MLA_KV_RANK ** -0.5),
        'mla_qk_q_norm': gain((DEPTH, MLA_QK_DIM)),
        'mla_qk_k_norm': gain((DEPTH, MLA_QK_DIM)),
        'mla_w_o': nrm((DEPTH, MLA_HEADS * MLA_V_DIM, D_MODEL), (MLA_HEADS * MLA_V_DIM) ** -0.5),
        'w_out': nrm((DEPTH, D_MODEL, D_MODEL), D_MODEL ** -0.5),
        'norm2_g': gain((DEPTH, D_MODEL)),
        'mem_norm_g': gain((DEPTH, D_MODEL)),
        'mem_w_q': nrm((DEPTH, D_MODEL, MEM_HEADS * MEM_HEAD_DIM), D_MODEL ** -0.5),
        'mem_w_kv': nrm((DEPTH, D_MODEL, 2 * MEM_HEADS * MEM_HEAD_DIM), D_MODEL ** -0.5),
        'mem_q_norm': gain((DEPTH, MEM_HEAD_DIM)),
        'mem_k_norm': gain((DEPTH, MEM_HEAD_DIM)),
        'mem_w_o': nrm((DEPTH, MEM_HEADS * MEM_HEAD_DIM, D_MODEL), (MEM_HEADS * MEM_HEAD_DIM) ** -0.5),
        'norm3_g': gain((DEPTH, D_MODEL)),
        'ffn_w_up': nrm((DEPTH, D_MODEL, 2 * D_FF), D_MODEL ** -0.5),
        'ffn_conv_w': nrm((DEPTH, CONV_W, D_FF), CONV_W ** -0.5),
        'ffn_conv_b': nrm((DEPTH, D_FF), 0.02),
        'ffn_w_down': nrm((DEPTH, D_FF, D_MODEL), D_FF ** -0.5),
    }


def reference(x_prompt, x_sample, cache_nsa, cache_mla, state_win, cache_mem, state_conv, page_table,
              mem_prompt, rel_bias, norm1_g, w_in, nsa_q_norm, nsa_k_norm, nsa_cmp_w1, nsa_cmp_b1,
              nsa_cmp_w2, nsa_w_o, mla_q_norm, mla_kv_norm, mla_w_uq, mla_w_ukv, mla_qk_q_norm,
              mla_qk_k_norm, mla_w_o, w_out, norm2_g, mem_norm_g, mem_w_q, mem_w_kv, mem_q_norm,
              mem_k_norm, mem_w_o, norm3_g, ffn_w_up, ffn_conv_w, ffn_conv_b, ffn_w_down):
    xp, xs = x_prompt, x_sample
    pos_p = jnp.arange(xp.shape[1], dtype=jnp.int32)
    pos_s = PAST_LEN + jnp.arange(xs.shape[1], dtype=jnp.int32)
    nsa_p, mla_p, win_p, mem_p, conv_p = [], [], [], [], []
    nsa_s, mla_s, win_s, conv_s = [], [], [], []
    for l in range(DEPTH):
        h = rmsnorm(xp, norm1_g[l])
        q_n, gates, nrows, wrows, qm, mrows, mg = project_mixer(
            h, pos_p, w_in[l], nsa_q_norm[l], nsa_k_norm[l], mla_q_norm[l], mla_kv_norm[l], mla_w_uq[l], mla_qk_q_norm[l])
        o_n, o_m = prompt_attention(q_n, gates, nrows, wrows, qm, mrows, nsa_cmp_w1[l], nsa_cmp_b1[l],
                                    nsa_cmp_w2[l], nsa_k_norm[l, 0], mla_w_ukv[l], mla_qk_k_norm[l], rel_bias)
        xp = xp + merge_out(mg, o_n, o_m, nsa_w_o[l], mla_w_o[l], w_out[l])
        nsa_p.append(nrows)
        mla_p.append(mrows)
        win_p.append(wrows[:, -min(WINDOW, xp.shape[1]):])
        mkv = mem_kv_rows(mem_prompt, mem_norm_g[l], mem_w_kv[l], mem_k_norm[l])
        xp = xp + mem_attend(rmsnorm(xp, norm2_g[l]), mkv, mem_w_q[l], mem_q_norm[l], mem_w_o[l])
        mem_p.append(mkv)
        zeros = jnp.zeros((xp.shape[0], CONV_W - 1, D_FF), xp.dtype)
        f, cst = conv_ffn(rmsnorm(xp, norm3_g[l]), zeros, ffn_w_up[l], ffn_conv_w[l], ffn_conv_b[l], ffn_w_down[l])
        xp = xp + f
        conv_p.append(cst)

        h = rmsnorm(xs, norm1_g[l])
        q_n, gates, nrows, wrows, qm, mrows, mg = project_mixer(
            h, pos_s, w_in[l], nsa_q_norm[l], nsa_k_norm[l], mla_q_norm[l], mla_kv_norm[l], mla_w_uq[l], mla_qk_q_norm[l])
        o_n, o_m, new_win = sample_attention(q_n, gates, nrows, wrows, qm, mrows, cache_nsa, cache_mla,
                                             state_win[l], page_table, l, nsa_cmp_w1[l], nsa_cmp_b1[l],
                                             nsa_cmp_w2[l], nsa_k_norm[l, 0], mla_w_ukv[l], mla_qk_k_norm[l], rel_bias)
        xs = xs + merge_out(mg, o_n, o_m, nsa_w_o[l], mla_w_o[l], w_out[l])
        nsa_s.append(nrows)
        mla_s.append(mrows)
        win_s.append(new_win)
        xs = xs + mem_attend(rmsnorm(xs, norm2_g[l]), cache_mem[l], mem_w_q[l], mem_q_norm[l], mem_w_o[l])
        f, cst = conv_ffn(rmsnorm(xs, norm3_g[l]), state_conv[l], ffn_w_up[l], ffn_conv_w[l], ffn_conv_b[l], ffn_w_down[l])
        xs = xs + f
        conv_s.append(cst)
    return (xp, xs, jnp.stack(nsa_p), jnp.stack(mla_p), jnp.stack(win_p), jnp.stack(mem_p), jnp.stack(conv_p),
            jnp.stack(nsa_s), jnp.stack(mla_s), jnp.stack(win_s), jnp.stack(conv_s))
```

```python
import functools
import math

import numpy as np
import jax
import jax.numpy as jnp
from jax import lax
from jax.experimental import pallas as pl
from jax.experimental.pallas import tpu as pltpu

D_MODEL = 1024
PAGE_SIZE = 128
NSA_HEADS = 8
NSA_GROUPS = 2
NSA_HPG = NSA_HEADS // NSA_GROUPS
NSA_HEAD_DIM = 64
CMP_BLOCK = 32
SEL_BLOCK = 64
TOP_N = 16
WINDOW = 512
MLA_HEADS = 8
MLA_Q_RANK = 384
MLA_KV_RANK = 256
MLA_NOPE_DIM = 64
MLA_ROPE_DIM = 32
MLA_QK_DIM = MLA_NOPE_DIM + MLA_ROPE_DIM
MLA_V_DIM = 64
ROPE_BASE = 10000.0
MEM_HEADS = 4
MEM_HEAD_DIM = 128
D_FF = 2816
CONV_W = 3
REL_BUCKETS = 32
REL_MAX_DIST = 128
NORM_EPS = 1e-6
NEG_INF = -1e30
FORCED_SCORE = 1e9

LANES = 128
SUBLANES = 8
VMEM_LIMIT = 56 * 1024 * 1024

HALF = LANES // 2
MLA_HEAD_PAD = LANES
ROPE_HALF = MLA_ROPE_DIM // 2

F32 = jnp.float32
BF16 = jnp.bfloat16


def _cparams(sem):
    return pltpu.CompilerParams(dimension_semantics=sem, vmem_limit_bytes=VMEM_LIMIT)


def _const_spec(shape):
    nd = len(shape)
    return pl.BlockSpec(shape, lambda *_: (0,) * nd)


def _dot(a, b):
    return jnp.dot(a, b, preferred_element_type=F32)


def _dot_t(a, b):
    return lax.dot_general(a, b, (((1,), (1,)), ((), ())), preferred_element_type=F32)


def _lane_iota(shape):
    return lax.broadcasted_iota(jnp.int32, shape, len(shape) - 1)


def _row_iota(shape):
    return lax.broadcasted_iota(jnp.int32, shape, len(shape) - 2)


def _rms_rows(x, gain):
    ms = jnp.mean(x * x, axis=-1, keepdims=True)
    return x * lax.rsqrt(ms + NORM_EPS) * gain


def _rms_half_lanes(x, gain):
    low = _lane_iota(x.shape) < HALF
    x2 = x * x
    s_lo = jnp.sum(jnp.where(low, x2, 0.0), axis=-1, keepdims=True)
    s_hi = jnp.sum(jnp.where(low, 0.0, x2), axis=-1, keepdims=True)
    ms = jnp.where(low, s_lo, s_hi) * (1.0 / HALF)
    return x * lax.rsqrt(ms + NORM_EPS) * gain


def _rope_block(x, c, s1, s2):
    up = pltpu.roll(x, LANES - ROPE_HALF, 1)
    dn = pltpu.roll(x, ROPE_HALF, 1)
    return x * c + up * s1 + dn * s2


def _rel_bucket_upper():
    exact = REL_BUCKETS // 2
    ub = []
    for b in range(REL_BUCKETS - 1):
        if b < exact:
            ub.append(b)
        else:
            d = b
            while True:
                nxt = d + 1
                lg = exact + int(math.log(nxt / exact) / math.log(REL_MAX_DIST / exact) * (REL_BUCKETS - exact))
                if min(lg, REL_BUCKETS - 1) > b:
                    break
                d = nxt
            ub.append(d)
    return ub


_BUCKET_UB = _rel_bucket_upper()


def _bias_from_dist(dist, rb_ref, head):
    out = jnp.full(dist.shape, rb_ref[REL_BUCKETS - 1, head], F32)
    for b in range(REL_BUCKETS - 2, -1, -1):
        out = jnp.where(dist <= _BUCKET_UB[b], rb_ref[b, head], out)
    return out


_QN_W = NSA_HEADS * NSA_HEAD_DIM
_KV_W = 6 * NSA_GROUPS * NSA_HEAD_DIM
_PROJ_W = _QN_W + _KV_W + MLA_Q_RANK + MLA_KV_RANK + 2 * LANES
_O_KV = _QN_W
_O_CQ = _O_KV + _KV_W
_O_CKV = _O_CQ + MLA_Q_RANK
_O_KR = _O_CKV + MLA_KV_RANK
_O_G = _O_KR + LANES
_MLA_PAD_W = MLA_HEADS * MLA_HEAD_PAD
_ROWS_W = 4 * NSA_GROUPS * NSA_HEAD_DIM
_WIN_W = 2 * NSA_GROUPS * NSA_HEAD_DIM
_MLA_ROW_W = MLA_KV_RANK + MLA_ROPE_DIM


def _proj_kernel(x_ref, tc_ref, ts1_ref, ts2_ref, g1_ref, w_ref, qg_ref, ksg_ref, kwg_ref,
                 cqg_ref, ckvg_ref, wuq_ref, qkq_ref, wkn_ref, qkk_ref,
                 qn_ref, rows_ref, win_ref, gate_ref, qm_ref, mrow_ref, kmla_ref, cb_ref):
    x = x_ref[...]
    h = _rms_rows(x, g1_ref[...]).astype(BF16)
    y = _dot(h, w_ref[...])

    nsa_scale = NSA_HEAD_DIM ** -0.5
    for v in range(_QN_W // LANES):
        blk = y[:, v * LANES:(v + 1) * LANES]
        qn_ref[:, v * LANES:(v + 1) * LANES] = (_rms_half_lanes(blk, qg_ref[...]) * nsa_scale).astype(BF16)

    kv = lambda slot: y[:, _O_KV + slot * LANES:_O_KV + (slot + 1) * LANES]
    rows_ref[:, 0:LANES] = kv(0)
    rows_ref[:, LANES:2 * LANES] = kv(1)
    rows_ref[:, 2 * LANES:3 * LANES] = _rms_half_lanes(kv(2), ksg_ref[...])
    rows_ref[:, 3 * LANES:4 * LANES] = kv(3)
    win_ref[:, 0:LANES] = _rms_half_lanes(kv(4), kwg_ref[...])
    win_ref[:, LANES:2 * LANES] = kv(5)
    gate_ref[...] = jax.nn.sigmoid(y[:, _O_G:_O_G + LANES])

    tc, ts1, ts2 = tc_ref[...], ts1_ref[...], ts2_ref[...]
    mla_scale = MLA_QK_DIM ** -0.5
    inv_qk = 1.0 / MLA_QK_DIM

    cq = _rms_rows(y[:, _O_CQ:_O_CQ + MLA_Q_RANK], cqg_ref[...]).astype(BF16)
    qraw = _dot(cq, wuq_ref[...])
    for hd in range(MLA_HEADS):
        blk = qraw[:, hd * LANES:(hd + 1) * LANES]
        ms = jnp.sum(blk * blk, axis=-1, keepdims=True) * inv_qk
        blk = blk * lax.rsqrt(ms + NORM_EPS) * qkq_ref[...]
        qm_ref[:, hd * LANES:(hd + 1) * LANES] = (_rope_block(blk, tc, ts1, ts2) * mla_scale).astype(BF16)

    c = _rms_rows(y[:, _O_CKV:_O_CKV + MLA_KV_RANK], ckvg_ref[...])
    krv = y[:, _O_KR:_O_KR + LANES]
    mrow_ref[:, 0:MLA_KV_RANK] = c
    mrow_ref[:, MLA_KV_RANK:_MLA_ROW_W] = krv[:, 0:MLA_ROPE_DIM]
    cbf = c.astype(BF16)
    cb_ref[...] = cbf

    kr_at = pltpu.roll(jnp.where(_lane_iota(krv.shape) < MLA_ROPE_DIM, krv, 0.0), MLA_NOPE_DIM, 1)
    ss_r = jnp.sum(kr_at * kr_at, axis=-1, keepdims=True)
    kr_rot = _rope_block(kr_at * qkk_ref[...], tc, ts1, ts2)
    kn = _dot(cbf, wkn_ref[...])
    for hd in range(MLA_HEADS):
        blk = kn[:, hd * LANES:(hd + 1) * LANES]
        ms = (jnp.sum(blk * blk, axis=-1, keepdims=True) + ss_r) * inv_qk
        kmla_ref[:, hd * LANES:(hd + 1) * LANES] = (
            (blk * qkk_ref[...] + kr_rot) * lax.rsqrt(ms + NORM_EPS)).astype(BF16)


def _rope_tables(pos):
    inv = jnp.power(ROPE_BASE, -jnp.arange(ROPE_HALF, dtype=F32) / ROPE_HALF)
    ang = pos.astype(F32)[:, None] * inv[None, :]
    cos, sin = jnp.cos(ang), jnp.sin(ang)
    n = pos.shape[0]
    z = lambda w: jnp.zeros((n, w), F32)
    pad = MLA_HEAD_PAD - MLA_QK_DIM
    tc = jnp.concatenate([jnp.ones((n, MLA_NOPE_DIM), F32), cos, cos, z(pad)], axis=1)
    ts1 = jnp.concatenate([z(MLA_NOPE_DIM), -sin, z(ROPE_HALF), z(pad)], axis=1)
    ts2 = jnp.concatenate([z(MLA_NOPE_DIM), z(ROPE_HALF), sin, z(pad)], axis=1)
    return tc, ts1, ts2


def _pad_heads(w, n_heads, width, used):
    k = w.shape[0]
    w = w.reshape(k, n_heads, width)[:, :, :used]
    return jnp.pad(w, ((0, 0), (0, 0), (0, MLA_HEAD_PAD - used))).reshape(k, n_heads * MLA_HEAD_PAD)


def _mla_gain_pad(g):
    return jnp.pad(g, (0, MLA_HEAD_PAD - MLA_QK_DIM)).reshape(1, MLA_HEAD_PAD)


def _proj_weights(w_in, nsa_q_norm, nsa_k_norm, mla_q_norm, mla_kv_norm, mla_w_uq, mla_qk_q_norm,
                  mla_w_ukv, mla_qk_k_norm):
    o = np.cumsum((0, _QN_W, _KV_W, 3 * NSA_HEADS, MLA_Q_RANK, MLA_KV_RANK, MLA_ROPE_DIM))
    qn = w_in[:, o[0]:o[1]].reshape(D_MODEL, NSA_GROUPS, NSA_HPG, NSA_HEAD_DIM)
    qn = jnp.transpose(qn, (0, 2, 1, 3)).reshape(D_MODEL, _QN_W)
    padc = lambda w: jnp.pad(w, ((0, 0), (0, LANES - w.shape[1])))
    w = jnp.concatenate([qn, w_in[:, o[1]:o[2]], w_in[:, o[3]:o[4]], w_in[:, o[4]:o[5]],
                         padc(w_in[:, o[5]:o[6]]), padc(w_in[:, o[2]:o[3]])], axis=1).astype(BF16)
    tile2 = lambda g: jnp.tile(g, NSA_GROUPS).reshape(1, LANES)
    wkn = mla_w_ukv.reshape(MLA_KV_RANK, MLA_HEADS, MLA_NOPE_DIM + MLA_V_DIM)
    wv = jnp.transpose(wkn[:, :, MLA_NOPE_DIM:], (1, 0, 2)).astype(BF16)
    wkn_flat = wkn[:, :, :MLA_NOPE_DIM].reshape(MLA_KV_RANK, MLA_HEADS * MLA_NOPE_DIM)
    return dict(
        w=w, qg=tile2(nsa_q_norm), ksg=tile2(nsa_k_norm[1]), kwg=tile2(nsa_k_norm[2]),
        cqg=mla_q_norm.reshape(1, -1), ckvg=mla_kv_norm.reshape(1, -1),
        wuq=_pad_heads(mla_w_uq, MLA_HEADS, MLA_QK_DIM, MLA_QK_DIM).astype(BF16),
        qkq=_mla_gain_pad(mla_qk_q_norm),
        wkn=_pad_heads(wkn_flat, MLA_HEADS, MLA_NOPE_DIM, MLA_NOPE_DIM).astype(BF16),
        qkk=_mla_gain_pad(mla_qk_k_norm), wv=wv, wkn_flat=wkn_flat.astype(BF16))


def _project(x, pos_tables, n_pos_tiles, g1, pw, tm):
    n = x.shape[0]
    grid = (n // tm,)
    row = lambda w: pl.BlockSpec((tm, w), lambda i: (i, 0))
    tab = pl.BlockSpec((tm, LANES), lambda i: (i % n_pos_tiles, 0))
    consts = [g1.reshape(1, -1), pw['w'], pw['qg'], pw['ksg'], pw['kwg'], pw['cqg'], pw['ckvg'],
              pw['wuq'], pw['qkq'], pw['wkn'], pw['qkk']]
    outs = pl.pallas_call(
        _proj_kernel,
        grid=grid,
        in_specs=[row(D_MODEL), tab, tab, tab] + [_const_spec(c.shape) for c in consts],
        out_specs=[row(_QN_W), row(_ROWS_W), row(_WIN_W), row(LANES), row(_MLA_PAD_W),
                   row(_MLA_ROW_W), row(_MLA_PAD_W), row(MLA_KV_RANK)],
        out_shape=[jax.ShapeDtypeStruct((n, _QN_W), BF16), jax.ShapeDtypeStruct((n, _ROWS_W), F32),
                   jax.ShapeDtypeStruct((n, _WIN_W), F32), jax.ShapeDtypeStruct((n, LANES), F32),
                   jax.ShapeDtypeStruct((n, _MLA_PAD_W), BF16), jax.ShapeDtypeStruct((n, _MLA_ROW_W), F32),
                   jax.ShapeDtypeStruct((n, _MLA_PAD_W), BF16), jax.ShapeDtypeStruct((n, MLA_KV_RANK), BF16)],
        compiler_params=_cparams(("arbitrary",)),
        name="proj",
    )(x, *pos_tables, *consts)
    return dict(zip(('qn', 'rows', 'win', 'gate', 'qm', 'mrow', 'kmla', 'cb'), outs))


_CMP_W = 2 * NSA_GROUPS * NSA_HEAD_DIM


def _compress_math(x_ref, w1_ref, b1_ref, w2_ref, kg_ref):
    acc = _dot(x_ref[:, 0, :].astype(BF16), w1_ref[0])
    for r in range(1, CMP_BLOCK):
        acc = acc + _dot(x_ref[:, r, :].astype(BF16), w1_ref[r])
    y = acc + b1_ref[...]
    y = y * jax.nn.sigmoid(y)
    z = _dot(y.astype(BF16), w2_ref[...])
    return jnp.concatenate([_rms_half_lanes(z[:, :LANES], kg_ref[...]), z[:, LANES:]], axis=1)


def _compress_kernel(x_ref, w1_ref, b1_ref, w2_ref, kg_ref, o_ref):
    o_ref[...] = _compress_math(x_ref, w1_ref, b1_ref, w2_ref, kg_ref)


def _compress_weights(w1, b1, w2, kc_norm):
    eye = jnp.eye(NSA_GROUPS, dtype=F32)
    w1r = w1.reshape(2, CMP_BLOCK, NSA_HEAD_DIM, NSA_HEAD_DIM)
    w1e = jnp.einsum('crde,cx,gy->rcgdxye', w1r, jnp.eye(2, dtype=F32), eye)
    w1e = w1e.reshape(CMP_BLOCK, _CMP_W, _CMP_W).astype(BF16)
    w2e = jnp.einsum('cde,cx,gy->cgdxye', w2, jnp.eye(2, dtype=F32), eye).reshape(_CMP_W, _CMP_W).astype(BF16)
    b1e = jnp.broadcast_to(b1[:, None, :], (2, NSA_GROUPS, NSA_HEAD_DIM)).reshape(1, _CMP_W)
    return dict(w1=w1e, b1=b1e, w2=w2e, kg=jnp.tile(kc_norm, NSA_GROUPS).reshape(1, LANES))


def _compress_rows(rows, cw, nbt):
    nb = rows.shape[0]
    consts = [cw['w1'], cw['b1'], cw['w2'], cw['kg']]
    return pl.pallas_call(
        _compress_kernel,
        grid=(nb // nbt,),
        in_specs=[pl.BlockSpec((nbt, CMP_BLOCK, _CMP_W), lambda i: (i, 0, 0))]
        + [_const_spec(c.shape) for c in consts],
        out_specs=pl.BlockSpec((nbt, _CMP_W), lambda i: (i, 0)),
        out_shape=jax.ShapeDtypeStruct((nb, _CMP_W), F32),
        compiler_params=_cparams(("arbitrary",)),
        name="compress",
    )(rows, *consts)


_TQ = 128
_NROW = NSA_HEADS * _TQ


def _stack_heads(q):
    low = _lane_iota((q.shape[0], LANES)) < HALF
    zero = jnp.zeros((), q.dtype)
    blocks = [q[:, v * LANES:(v + 1) * LANES] for v in range(NSA_HPG)]
    return jnp.concatenate([jnp.where(low, b, zero) for b in blocks]
                           + [jnp.where(low, zero, b) for b in blocks], axis=0)


def _unstack_heads(o, tq):
    low = _lane_iota((tq, LANES)) < HALF
    return jnp.concatenate(
        [jnp.where(low, o[v * tq:(v + 1) * tq], o[(NSA_HPG + v) * tq:(NSA_HPG + v + 1) * tq])
         for v in range(NSA_HPG)], axis=1)


def _exact_dot01(x, m01):
    a = x.astype(BF16)
    r1 = x - a.astype(F32)
    b = r1.astype(BF16)
    c = (r1 - b.astype(F32)).astype(BF16)
    return _dot(a, m01) + _dot(b, m01) + _dot(c, m01)


def _top_blocks(imp, blk, n_pick):
    big = imp.shape[-1]
    sel = jnp.zeros(imp.shape, F32)
    x = imp
    for _ in range(n_pick):
        m = jnp.max(x, axis=-1, keepdims=True)
        idx = jnp.min(jnp.where(x == m, blk, big), axis=-1, keepdims=True)
        hit = blk == idx
        sel = jnp.where(hit, 1.0, sel)
        x = jnp.where(hit, -3e38, x)
    return sel


def _flash_step(s, allowed, v, m_ref, l_ref, acc_ref):
    s = jnp.where(allowed, s, NEG_INF)
    m_old = m_ref[...]
    m_new = jnp.maximum(m_old, jnp.max(s, axis=-1, keepdims=True))
    alpha = jnp.exp(m_old - m_new)
    p = jnp.exp(s - m_new)
    l_ref[...] = alpha * l_ref[...] + jnp.sum(p, axis=-1, keepdims=True)
    acc_ref[...] = alpha * acc_ref[...] + _dot(p.astype(BF16), v)
    m_ref[...] = m_new


def _nsa_prompt_kernel(rb_ref, q_ref, gate_ref, kc_ref, sel_ref, win_ref, o_ref,
                       bias_ref, ms_ref, ls_ref, as_ref, mw_ref, lw_ref, aw_ref, *, n_sel, n_pick):
    b = pl.program_id(0)
    qi = pl.program_id(1)
    tq = _TQ
    t_loc = _row_iota((tq, LANES))
    k_loc = _lane_iota((tq, LANES))

    @pl.when((b == 0) & (qi == 0))
    def _():
        for kind in range(3):
            dist = t_loc - k_loc + kind * tq
            for hd in range(NSA_HEADS):
                bias_ref[kind, hd * tq:(hd + 1) * tq, :] = _bias_from_dist(dist, rb_ref, hd)

    q8 = _stack_heads(q_ref[...])
    t_pos = qi * tq + t_loc

    kc = kc_ref[...]
    n_cmp = kc.shape[0]
    s_c = _dot_t(q8, kc[:, :LANES].astype(BF16))
    t_pos_c = qi * tq + _row_iota((tq, n_cmp))
    dist_c = t_pos_c - (_lane_iota((tq, n_cmp)) * CMP_BLOCK + (CMP_BLOCK - 1))
    valid_c = jnp.concatenate([dist_c >= 0] * NSA_HEADS, axis=0)
    bias_c = jnp.concatenate([_bias_from_dist(dist_c, rb_ref, hd) for hd in range(NSA_HEADS)], axis=0)
    s_c = jnp.where(valid_c, s_c + bias_c, NEG_INF)
    e_c = jnp.exp(s_c - jnp.max(s_c, axis=-1, keepdims=True))
    p_c = jnp.where(valid_c, e_c / jnp.sum(e_c, axis=-1, keepdims=True), 0.0)
    o_c = _dot(p_c.astype(BF16), kc[:, LANES:].astype(BF16))

    pair = (_row_iota((n_cmp, n_sel)) // (SEL_BLOCK // CMP_BLOCK) == _lane_iota((n_cmp, n_sel)))
    pair = jnp.where(pair, 1.0, 0.0).astype(BF16)
    blk = _lane_iota((tq, n_sel))
    cur = (qi * tq + _row_iota((tq, n_sel))) // SEL_BLOCK
    forced = (blk == 0) | (blk == cur) | (blk == cur - 1)
    sels = []
    for g in range(NSA_GROUPS):
        imp_n = p_c[(g * NSA_HPG) * tq:(g * NSA_HPG + 1) * tq]
        for hl in range(1, NSA_HPG):
            imp_n = imp_n + p_c[(g * NSA_HPG + hl) * tq:(g * NSA_HPG + hl + 1) * tq]
        imp = _exact_dot01(imp_n, pair)
        imp = jnp.where(forced, FORCED_SCORE, jnp.where(blk <= cur, imp, -1.0))
        sels.append(_top_blocks(imp, blk, n_pick).astype(BF16))

    for m_ref, l_ref, a_ref in ((ms_ref, ls_ref, as_ref), (mw_ref, lw_ref, aw_ref)):
        m_ref[...] = jnp.full(m_ref.shape, NEG_INF, F32)
        l_ref[...] = jnp.zeros(l_ref.shape, F32)
        a_ref[...] = jnp.zeros(a_ref.shape, F32)

    def tile_common(kt, src_ref):
        rows = src_ref[pl.ds(pl.multiple_of(kt * tq, tq), tq), :]
        kind = jnp.minimum(qi - kt, 2)
        s = _dot_t(q8, rows[:, :LANES].astype(BF16)) + bias_ref[kind]
        dist = t_pos - (kt * tq + k_loc)
        return s, rows[:, LANES:].astype(BF16), dist

    def sel_tile(kt, carry):
        s, v, dist = tile_common(kt, sel_ref)
        expand = (_row_iota((n_sel, tq)) == kt * (tq // SEL_BLOCK) + _lane_iota((n_sel, tq)) // SEL_BLOCK)
        expand = jnp.where(expand, 1.0, 0.0).astype(BF16)
        ok = [(_dot(sels[g], expand) > 0.5) & (dist >= 0) for g in range(NSA_GROUPS)]
        allowed = jnp.concatenate([ok[0]] * NSA_HPG + [ok[1]] * NSA_HPG, axis=0)
        _flash_step(s, allowed, v, ms_ref, ls_ref, as_ref)
        return carry

    def win_tile(kt, carry):
        s, v, dist = tile_common(kt, win_ref)
        ok = (dist >= 0) & (dist < WINDOW)
        _flash_step(s, jnp.concatenate([ok] * NSA_HEADS, axis=0), v, mw_ref, lw_ref, aw_ref)
        return carry

    lax.fori_loop(0, qi + 1, sel_tile, 0)
    lax.fori_loop(jnp.maximum(qi - WINDOW // tq, 0), qi + 1, win_tile, 0)

    o_s = as_ref[...] / ls_ref[...]
    o_w = aw_ref[...] / lw_ref[...]
    gate = gate_ref[...]
    outs = []
    for hd in range(NSA_HEADS):
        sl = slice(hd * tq, (hd + 1) * tq)
        outs.append(gate[:, 3 * hd:3 * hd + 1] * o_c[sl] + gate[:, 3 * hd + 1:3 * hd + 2] * o_s[sl]
                    + gate[:, 3 * hd + 2:3 * hd + 3] * o_w[sl])
    o_ref[...] = _unstack_heads(jnp.concatenate(outs, axis=0), tq).astype(o_ref.dtype)


def _nsa_prompt(qn, gate, kc, rows, win, rel_bias):
    bsz, t, _ = qn.shape
    n_cmp = t // CMP_BLOCK
    n_sel = t // SEL_BLOCK
    kern = functools.partial(_nsa_prompt_kernel, n_sel=n_sel, n_pick=min(TOP_N, n_sel))
    stat = lambda w: pltpu.VMEM((_NROW, w), F32)
    return pl.pallas_call(
        kern,
        grid=(bsz, t // _TQ),
        in_specs=[pl.BlockSpec(memory_space=pltpu.SMEM),
                  pl.BlockSpec((None, _TQ, _QN_W), lambda b, i: (b, i, 0)),
                  pl.BlockSpec((None, _TQ, LANES), lambda b, i: (b, i, 0)),
                  pl.BlockSpec((None, n_cmp, _CMP_W), lambda b, i: (b, 0, 0)),
                  pl.BlockSpec((None, t, _WIN_W), lambda b, i: (b, 0, 1)),
                  pl.BlockSpec((None, t, _WIN_W), lambda b, i: (b, 0, 0))],
        out_specs=pl.BlockSpec((None, _TQ, _QN_W), lambda b, i: (b, i, 0)),
        out_shape=jax.ShapeDtypeStruct((bsz, t, _QN_W), BF16),
        scratch_shapes=[pltpu.VMEM((3, _NROW, LANES), F32),
                        stat(1), stat(1), stat(LANES), stat(1), stat(1), stat(LANES)],
        compiler_params=_cparams(("arbitrary", "arbitrary")),
        name="nsa_prompt",
    )(rel_bias, qn, gate, kc, rows, win)


_TM = 256


def _mla_prompt_kernel(q_ref, k_ref, c_ref, wv_ref, o_ref, m_ref, l_ref, acc_ref):
    qi = pl.program_id(1)
    tq = tk = _TM
    m_ref[...] = jnp.full(m_ref.shape, NEG_INF, F32)
    l_ref[...] = jnp.zeros(l_ref.shape, F32)
    acc_ref[...] = jnp.zeros(acc_ref.shape, F32)
    t_loc = _row_iota((tq, tk))
    k_loc = _lane_iota((tq, tk))

    def tile(kt, carry):
        start = pl.multiple_of(kt * tk, tk)
        c = c_ref[pl.ds(start, tk), :]
        allowed = (qi - kt) * tq + t_loc >= k_loc
        for hd in range(MLA_HEADS):
            lanes = slice(hd * LANES, (hd + 1) * LANES)
            s = _dot_t(q_ref[:, lanes], k_ref[pl.ds(start, tk), lanes])
            s = jnp.where(allowed, s, NEG_INF)
            m_old = m_ref[hd]
            m_new = jnp.maximum(m_old, jnp.max(s, axis=-1, keepdims=True))
            alpha = jnp.exp(m_old - m_new)
            p = jnp.exp(s - m_new)
            l_ref[hd] = alpha * l_ref[hd] + jnp.sum(p, axis=-1, keepdims=True)
            acc_ref[hd] = alpha * acc_ref[hd] + _dot(p.astype(BF16), c)
            m_ref[hd] = m_new
        return carry

    lax.fori_loop(0, qi + 1, tile, 0)
    outs = [_dot((acc_ref[hd] / l_ref[hd]).astype(BF16), wv_ref[hd]) for hd in range(MLA_HEADS)]
    o_ref[...] = jnp.concatenate(outs, axis=1).astype(o_ref.dtype)


def _mla_prompt(qm, kmla, cb, wv):
    bsz, t, _ = qm.shape
    ow = MLA_HEADS * MLA_V_DIM
    return pl.pallas_call(
        _mla_prompt_kernel,
        grid=(bsz, t // _TM),
        in_specs=[pl.BlockSpec((None, _TM, _MLA_PAD_W), lambda b, i: (b, i, 0)),
                  pl.BlockSpec((None, t, _MLA_PAD_W), lambda b, i: (b, 0, 0)),
                  pl.BlockSpec((None, t, MLA_KV_RANK), lambda b, i: (b, 0, 0)),
                  _const_spec(wv.shape)],
        out_specs=pl.BlockSpec((None, _TM, ow), lambda b, i: (b, i, 0)),
        out_shape=jax.ShapeDtypeStruct((bsz, t, ow), BF16),
        scratch_shapes=[pltpu.VMEM((MLA_HEADS, _TM, 1), F32), pltpu.VMEM((MLA_HEADS, _TM, 1), F32),
                        pltpu.VMEM((MLA_HEADS, _TM, MLA_KV_RANK), F32)],
        compiler_params=_cparams(("arbitrary", "arbitrary")),
        name="mla_prompt",
    )(qm, kmla, cb, wv)


_MEM_W = MEM_HEADS * MEM_HEAD_DIM


def _merge_kernel(x_ref, on_ref, om_ref, g1_ref, wmg_ref, wno_ref, wmo_ref, wout_ref,
                  g2_ref, wq_ref, qg_ref, x1_ref, qmem_ref):
    x = x_ref[...]
    h = _rms_rows(x, g1_ref[...]).astype(BF16)
    mg = jax.nn.sigmoid(_dot(h, wmg_ref[...]))
    a = _dot(on_ref[...], wno_ref[...])
    b = _dot(om_ref[...], wmo_ref[...])
    r = mg[:, :D_MODEL] * a + mg[:, D_MODEL:] * b
    x1 = x + _dot(r.astype(BF16), wout_ref[...])
    x1_ref[...] = x1
    q = _dot(_rms_rows(x1, g2_ref[...]).astype(BF16), wq_ref[...])
    scale = MEM_HEAD_DIM ** -0.5
    for hd in range(MEM_HEADS):
        blk = q[:, hd * LANES:(hd + 1) * LANES]
        qmem_ref[:, hd * LANES:(hd + 1) * LANES] = (_rms_rows(blk, qg_ref[...]) * scale).astype(BF16)


def _merge(x, o_n, o_m, g1, wmg, wno, wmo, wout, g2, wq, qg, tm):
    n = x.shape[0]
    row = lambda w: pl.BlockSpec((tm, w), lambda i: (i, 0))
    consts = [g1.reshape(1, -1), wmg, wno, wmo, wout, g2.reshape(1, -1), wq, qg.reshape(1, -1)]
    return pl.pallas_call(
        _merge_kernel,
        grid=(n // tm,),
        in_specs=[row(D_MODEL), row(_QN_W), row(MLA_HEADS * MLA_V_DIM)] + [_const_spec(c.shape) for c in consts],
        out_specs=[row(D_MODEL), row(_MEM_W)],
        out_shape=[jax.ShapeDtypeStruct((n, D_MODEL), F32), jax.ShapeDtypeStruct((n, _MEM_W), BF16)],
        compiler_params=_cparams(("arbitrary",)),
        name="merge",
    )(x, o_n, o_m, *consts)


def _mem_kv_kernel(m_ref, g_ref, w_ref, kg_ref, o_ref):
    kv = _dot(_rms_rows(m_ref[...], g_ref[...]).astype(BF16), w_ref[...])
    for hd in range(MEM_HEADS):
        o_ref[:, hd * LANES:(hd + 1) * LANES] = _rms_rows(kv[:, hd * LANES:(hd + 1) * LANES], kg_ref[...])
    o_ref[:, _MEM_W:] = kv[:, _MEM_W:]


def _mem_kv(mem, g, w, kg):
    n = mem.shape[0]
    tm = min(n, _TM)
    consts = [g.reshape(1, -1), w, kg.reshape(1, -1)]
    return pl.pallas_call(
        _mem_kv_kernel,
        grid=(n // tm,),
        in_specs=[pl.BlockSpec((tm, D_MODEL), lambda i: (i, 0))] + [_const_spec(c.shape) for c in consts],
        out_specs=pl.BlockSpec((tm, 2 * _MEM_W), lambda i: (i, 0)),
        out_shape=jax.ShapeDtypeStruct((n, 2 * _MEM_W), F32),
        compiler_params=_cparams(("arbitrary",)),
        name="mem_kv",
    )(mem, *consts)


def _mem_attn_kernel(q_ref, kv_ref, o_ref, *, n_seq):
    tq = q_ref.shape[0] // n_seq
    for s in range(n_seq):
        outs = []
        for hd in range(MEM_HEADS):
            q = q_ref[s * tq:(s + 1) * tq, hd * LANES:(hd + 1) * LANES]
            sc = _dot_t(q, kv_ref[s, :, hd * LANES:(hd + 1) * LANES].astype(BF16))
            e = jnp.exp(sc - jnp.max(sc, axis=-1, keepdims=True))
            p = e / jnp.sum(e, axis=-1, keepdims=True)
            outs.append(_dot(p.astype(BF16), kv_ref[s, :, _MEM_W + hd * LANES:_MEM_W + (hd + 1) * LANES].astype(BF16)))
        o_ref[s * tq:(s + 1) * tq, :] = jnp.concatenate(outs, axis=1).astype(o_ref.dtype)


def _mem_attn(q, kv, tq, n_seq):
    n = q.shape[0]
    per_kv = n // kv.shape[0]
    rows = tq * n_seq
    if n_seq == 1:
        kv_map = lambda i: (i * tq // per_kv, 0, 0)
    else:
        kv_map = lambda i: (i, 0, 0)
    return pl.pallas_call(
        functools.partial(_mem_attn_kernel, n_seq=n_seq),
        grid=(n // rows,),
        in_specs=[pl.BlockSpec((rows, _MEM_W), lambda i: (i, 0)),
                  pl.BlockSpec((n_seq,) + kv.shape[1:], kv_map)],
        out_specs=pl.BlockSpec((rows, _MEM_W), lambda i: (i, 0)),
        out_shape=jax.ShapeDtypeStruct((n, _MEM_W), BF16),
        compiler_params=_cparams(("arbitrary",)),
        name="mem_attn",
    )(q, kv)


_FF_CHUNK = 256
_CARRY_ROWS = SUBLANES


def _ffn_kernel(x1_ref, om_ref, wo_ref, g3_ref, wup_ref, cw_ref, cb_ref, wdn_ref, past_ref,
                y_ref, a_ref, carry_ref, *, seq_tiles, seq_len):
    i = pl.program_id(0)
    tm = x1_ref.shape[0]
    x2 = x1_ref[...] + _dot(om_ref[...], wo_ref[...])
    h = _rms_rows(x2, g3_ref[...]).astype(BF16)
    short = seq_len < tm
    row = _row_iota((tm, _FF_CHUNK))
    rin = row % seq_len if short else row
    first = (i % seq_tiles) == 0
    acc = jnp.zeros((tm, D_MODEL), F32)
    for c in range(D_FF // _FF_CHUNK):
        cols = slice(c * _FF_CHUNK, (c + 1) * _FF_CHUNK)
        a = _dot(h, wup_ref[:, cols])
        v = _dot(h, wup_ref[:, D_FF + c * _FF_CHUNK:D_FF + (c + 1) * _FF_CHUNK])
        if short:
            prev = past_ref[:, cols]
            a1 = jnp.where(rin == 0, pltpu.roll(prev, tm - 1, 0), pltpu.roll(a, 1, 0))
            a2 = jnp.where(rin < 2, prev, pltpu.roll(a, 2, 0))
            a_ref[:, cols] = a
        else:
            prev = jnp.where(first, 0.0, carry_ref[:, cols])
            prev = jnp.concatenate([prev] * (tm // _CARRY_ROWS), axis=0)
            a1 = jnp.where(rin == 0, pltpu.roll(prev, 1, 0), pltpu.roll(a, 1, 0))
            a2 = jnp.where(rin < 2, pltpu.roll(prev, 2, 0), pltpu.roll(a, 2, 0))
            carry_ref[:, cols] = a[tm - _CARRY_ROWS:, :]
            a_ref[:, cols] = a[tm - _CARRY_ROWS:, :]
        cv = cb_ref[:, cols] + cw_ref[0:1, cols] * a2 + cw_ref[1:2, cols] * a1 + cw_ref[2:3, cols] * a
        act = (cv * jax.nn.sigmoid(cv) * v).astype(BF16)
        acc = acc + _dot(act, wdn_ref[cols, :])
    y_ref[...] = x2 + acc


def _ffn(x1, o_mem, wo, g3, wup, cw, cb, wdn, past, tm, seq_len):
    n = x1.shape[0]
    short = seq_len < tm
    row = lambda w: pl.BlockSpec((tm, w), lambda i: (i, 0))
    consts = [wo, g3.reshape(1, -1), wup, cw, cb.reshape(1, -1), wdn]
    if short:
        past_spec, a_rows, a_spec = row(D_FF), n, row(D_FF)
    else:
        past = jnp.zeros((_CARRY_ROWS, D_FF), F32)
        past_spec = _const_spec(past.shape)
        a_rows = n // tm * _CARRY_ROWS
        a_spec = pl.BlockSpec((_CARRY_ROWS, D_FF), lambda i: (i, 0))
    kern = functools.partial(_ffn_kernel, seq_tiles=max(seq_len // tm, 1), seq_len=seq_len)
    return pl.pallas_call(
        kern,
        grid=(n // tm,),
        in_specs=[row(D_MODEL), row(_MEM_W)] + [_const_spec(c.shape) for c in consts] + [past_spec],
        out_specs=[row(D_MODEL), a_spec],
        out_shape=[jax.ShapeDtypeStruct((n, D_MODEL), F32), jax.ShapeDtypeStruct((a_rows, D_FF), F32)],
        scratch_shapes=[pltpu.VMEM((_CARRY_ROWS, D_FF), F32)],
        compiler_params=_cparams(("arbitrary",)),
        name="ffn",
    )(x1, o_mem, *consts, past)


def _paged_loop(pt_ref, n_chunks, pages, page_copy, body):
    s = pl.program_id(0)
    n_seq = pl.num_programs(0)

    def start(seq, chunk, slot):
        for i in range(pages):
            page_copy(pt_ref[seq, chunk * pages + i], slot, i).start()

    @pl.when(s == 0)
    def _():
        start(0, 0, 0)

    def step(j, carry):
        g = s * n_chunks + j
        slot = g % 2
        for i in range(pages):
            page_copy(0, slot, i).wait()

        @pl.when(j + 1 < n_chunks)
        def _():
            start(s, j + 1, 1 - slot)

        @pl.when((j + 1 == n_chunks) & (s + 1 < n_seq))
        def _():
            start(s + 1, 0, 1 - slot)

        body(j, slot)
        return carry

    lax.fori_loop(0, n_chunks, step, 0)


def _seq_grid_spec(n_seq, in_specs, out_specs, scratch):
    return pltpu.PrefetchScalarGridSpec(num_scalar_prefetch=1, grid=(n_seq,), in_specs=in_specs,
                                        out_specs=out_specs, scratch_shapes=scratch)


_CMP_PAGES = 32
_BLK_PER_PAGE = PAGE_SIZE // CMP_BLOCK
_TAIL_BLOCKS = SUBLANES


def _compress_paged_kernel(pt_ref, cache_ref, tail_ref, w1_ref, b1_ref, w2_ref, kg_ref, o_ref,
                           buf_ref, sem_ref, *, n_chunks, pages):
    def page_copy(page, slot, i):
        return pltpu.make_async_copy(
            cache_ref.at[page, :, :, pl.ds(0, _CMP_W)],
            buf_ref.at[slot, pl.ds(i * _BLK_PER_PAGE, _BLK_PER_PAGE)], sem_ref.at[slot])

    nb = pages * _BLK_PER_PAGE

    def body(j, slot):
        o_ref[pl.ds(pl.multiple_of(j * nb, nb), nb), :] = _compress_math(
            buf_ref.at[slot], w1_ref, b1_ref, w2_ref, kg_ref)

    n_past = n_chunks * nb
    o_ref[n_past:n_past + _TAIL_BLOCKS, :] = _compress_math(tail_ref, w1_ref, b1_ref, w2_ref, kg_ref)
    _paged_loop(pt_ref, n_chunks, pages, page_copy, body)


def _compress_paged(page_table, cache, tail, cw):
    n_seq, n_pages = page_table.shape
    pages = min(_CMP_PAGES, n_pages)
    n_chunks = n_pages // pages
    n_out = n_pages * _BLK_PER_PAGE + _TAIL_BLOCKS
    consts = [cw['w1'], cw['b1'], cw['w2'], cw['kg']]
    cs = lambda c: pl.BlockSpec(c.shape, lambda s, pt: (0,) * c.ndim)
    kern = functools.partial(_compress_paged_kernel, n_chunks=n_chunks, pages=pages)
    return pl.pallas_call(
        kern,
        grid_spec=_seq_grid_spec(
            n_seq,
            [pl.BlockSpec(memory_space=pl.ANY),
             pl.BlockSpec((None, _TAIL_BLOCKS, CMP_BLOCK, _CMP_W), lambda s, pt: (s, 0, 0, 0))]
            + [cs(c) for c in consts],
            pl.BlockSpec((None, n_out, _CMP_W), lambda s, pt: (s, 0, 0)),
            [pltpu.VMEM((2, pages * _BLK_PER_PAGE, CMP_BLOCK, _CMP_W), F32), pltpu.SemaphoreType.DMA((2,))]),
        out_shape=jax.ShapeDtypeStruct((n_seq, n_out, _CMP_W), F32),
        compiler_params=_cparams(("arbitrary",)),
        name="compress_paged",
    )(page_table, cache, tail, *consts)


_SEL_PAGES = 16


def _nsa_sample_kernel(pt_ref, rb_ref, q_ref, gate_ref, kc_ref, cache_ref, new_ref, wst_ref, wnew_ref,
                       o_ref, buf_ref, sem_ref, ms_ref, ls_ref, as_ref, mw_ref, lw_ref, aw_ref,
                       *, n_chunks, pages, past_len, n_cmp_real, n_pick):
    tq = q_ref.shape[0]
    nrow = NSA_HEADS * tq
    kp = pages * PAGE_SIZE

    def page_copy(page, slot, i):
        return pltpu.make_async_copy(
            cache_ref.at[page, :, pl.ds(_WIN_W, _WIN_W)],
            buf_ref.at[slot, pl.ds(i * PAGE_SIZE, PAGE_SIZE)], sem_ref.at[slot])

    q8 = _stack_heads(q_ref[...])

    def head_bias(dist):
        return jnp.concatenate([_bias_from_dist(dist, rb_ref, hd) for hd in range(NSA_HEADS)], axis=0)

    def rep_heads(x):
        return jnp.concatenate([x] * NSA_HEADS, axis=0)

    kc = kc_ref[...]
    n_cmp = kc.shape[0]
    n_sel = n_cmp // (SEL_BLOCK // CMP_BLOCK)
    q_pos_c = past_len + _row_iota((tq, n_cmp))
    blk_c = _lane_iota((tq, n_cmp))
    dist_c = q_pos_c - (blk_c * CMP_BLOCK + (CMP_BLOCK - 1))
    ok_c = rep_heads((dist_c >= 0) & (blk_c < n_cmp_real))
    s_c = jnp.where(ok_c, _dot_t(q8, kc[:, :LANES].astype(BF16)) + head_bias(dist_c), NEG_INF)
    e_c = jnp.exp(s_c - jnp.max(s_c, axis=-1, keepdims=True))
    p_c = jnp.where(ok_c, e_c / jnp.sum(e_c, axis=-1, keepdims=True), 0.0)
    o_c = _dot(p_c.astype(BF16), kc[:, LANES:].astype(BF16))

    pair = (_row_iota((n_cmp, n_sel)) // (SEL_BLOCK // CMP_BLOCK) == _lane_iota((n_cmp, n_sel)))
    pair = jnp.where(pair, 1.0, 0.0).astype(BF16)
    blk = _lane_iota((tq, n_sel))
    cur = (past_len + _row_iota((tq, n_sel))) // SEL_BLOCK
    forced = (blk == 0) | (blk == cur) | (blk == cur - 1)
    picked = []
    for g in range(NSA_GROUPS):
        imp_n = p_c[(g * NSA_HPG) * tq:(g * NSA_HPG + 1) * tq]
        for hl in range(1, NSA_HPG):
            imp_n = imp_n + p_c[(g * NSA_HPG + hl) * tq:(g * NSA_HPG + hl + 1) * tq]
        x = _exact_dot01(imp_n, pair)
        x = jnp.where(forced, FORCED_SCORE, jnp.where(blk <= cur, x, -1.0))
        ids = []
        for _ in range(n_pick):
            m = jnp.max(x, axis=-1, keepdims=True)
            idx = jnp.min(jnp.where(x == m, blk, n_sel), axis=-1, keepdims=True)
            ids.append(idx)
            x = jnp.where(blk == idx, -3e38, x)
        picked.append(ids)

    for m_ref, l_ref, a_ref in ((ms_ref, ls_ref, as_ref), (mw_ref, lw_ref, aw_ref)):
        m_ref[...] = jnp.full(m_ref.shape, NEG_INF, F32)
        l_ref[...] = jnp.zeros(l_ref.shape, F32)
        a_ref[...] = jnp.zeros(a_ref.shape, F32)

    def picked_mask(key_blk):
        per_group = []
        for g in range(NSA_GROUPS):
            hit = jnp.zeros(key_blk.shape, F32)
            for idx in picked[g]:
                hit = jnp.where(key_blk == idx, 1.0, hit)
            per_group.append(hit > 0.5)
        return jnp.concatenate([per_group[0]] * NSA_HPG + [per_group[1]] * NSA_HPG, axis=0)

    wst = wst_ref[...]
    w_s = wst.shape[0]
    dist_w = w_s + _row_iota((tq, w_s)) - _lane_iota((tq, w_s))
    _flash_step(_dot_t(q8, wst[:, :LANES].astype(BF16)) + head_bias(dist_w),
                rep_heads((dist_w >= 0) & (dist_w < WINDOW)), wst[:, LANES:].astype(BF16), mw_ref, lw_ref, aw_ref)
    dist_n = _row_iota((tq, tq)) - _lane_iota((tq, tq))
    bias_n = head_bias(dist_n)
    ok_n = rep_heads(dist_n >= 0)
    wnew = wnew_ref[...]
    _flash_step(_dot_t(q8, wnew[:, :LANES].astype(BF16)) + bias_n, ok_n, wnew[:, LANES:].astype(BF16),
                mw_ref, lw_ref, aw_ref)

    new = new_ref[...]
    cur_new = (past_len + _lane_iota((tq, tq))) // SEL_BLOCK
    _flash_step(_dot_t(q8, new[:, :LANES].astype(BF16)) + bias_n, ok_n & picked_mask(cur_new),
                new[:, LANES:].astype(BF16), ms_ref, ls_ref, as_ref)

    far = head_bias(jnp.full((tq, PAGE_SIZE), REL_MAX_DIST, jnp.int32))
    near = head_bias(past_len + _row_iota((tq, PAGE_SIZE)) - (past_len - PAGE_SIZE + _lane_iota((tq, PAGE_SIZE))))

    def body(j, slot):
        rows = buf_ref[slot]
        last = jnp.where(j == n_chunks - 1, near, far)
        bias = jnp.concatenate([far] * (pages - 1) + [last], axis=1)
        key_blk = (j * kp + _lane_iota((tq, kp))) // SEL_BLOCK
        _flash_step(_dot_t(q8, rows[:, :LANES].astype(BF16)) + bias, picked_mask(key_blk),
                    rows[:, LANES:].astype(BF16), ms_ref, ls_ref, as_ref)

    _paged_loop(pt_ref, n_chunks, pages, page_copy, body)

    o_s = as_ref[...] / ls_ref[...]
    o_w = aw_ref[...] / lw_ref[...]
    gate = gate_ref[...]
    outs = []
    for hd in range(NSA_HEADS):
        sl = slice(hd * tq, (hd + 1) * tq)
        outs.append(gate[:, 3 * hd:3 * hd + 1] * o_c[sl] + gate[:, 3 * hd + 1:3 * hd + 2] * o_s[sl]
                    + gate[:, 3 * hd + 2:3 * hd + 3] * o_w[sl])
    o_ref[...] = _unstack_heads(jnp.concatenate(outs, axis=0), tq).astype(o_ref.dtype)


def _nsa_sample(page_table, rel_bias, qn, gate, kc, cache, rows_new, win_state, win_new, past_len):
    n_seq, n_pages = page_table.shape
    tq = qn.shape[1]
    pages = min(_SEL_PAGES, n_pages)
    n_chunks = n_pages // pages
    n_cmp = kc.shape[1]
    n_cmp_real = -(-(past_len + tq) // SEL_BLOCK) * SEL_BLOCK // CMP_BLOCK
    n_sel_real = n_cmp_real // (SEL_BLOCK // CMP_BLOCK)
    nrow = NSA_HEADS * tq
    kern = functools.partial(_nsa_sample_kernel, n_chunks=n_chunks, pages=pages, past_len=past_len,
                             n_cmp_real=n_cmp_real, n_pick=min(TOP_N, n_sel_real))
    per_seq = lambda shp, last=0: pl.BlockSpec((None,) + shp, lambda s, pt: (s, 0, last))
    stat = lambda w: pltpu.VMEM((nrow, w), F32)
    return pl.pallas_call(
        kern,
        grid_spec=_seq_grid_spec(
            n_seq,
            [pl.BlockSpec(memory_space=pltpu.SMEM), per_seq((tq, _QN_W)), per_seq((tq, LANES)),
             per_seq((n_cmp, _CMP_W)), pl.BlockSpec(memory_space=pl.ANY),
             per_seq((tq, _WIN_W), 1), per_seq((win_state.shape[1], _WIN_W)), per_seq((tq, _WIN_W))],
            per_seq((tq, _QN_W)),
            [pltpu.VMEM((2, pages * PAGE_SIZE, _WIN_W), F32), pltpu.SemaphoreType.DMA((2,)),
             stat(1), stat(1), stat(LANES), stat(1), stat(1), stat(LANES)]),
        out_shape=jax.ShapeDtypeStruct((n_seq, tq, _QN_W), BF16),
        compiler_params=_cparams(("arbitrary",)),
        name="nsa_sample",
    )(page_table, rel_bias, qn, gate, kc, cache, rows_new, win_state, win_new)


_MLA_PAGES = 16


def _mla_sample_kernel(pt_ref, q_ref, cache_ref, new_ref, tab_ref, tabn_ref, wknt_ref, wv_ref, gk_ref,
                       o_ref, buf_ref, sem_ref, m_ref, l_ref, acc_ref, *, n_chunks, pages):
    tq = o_ref.shape[0]
    nrow = MLA_HEADS * tq
    kp = pages * PAGE_SIZE

    def page_copy(page, slot, i):
        return pltpu.make_async_copy(
            cache_ref.at[page], buf_ref.at[slot, pl.ds(i * PAGE_SIZE, PAGE_SIZE)], sem_ref.at[slot])

    q = q_ref[...].astype(F32)
    lane = _lane_iota(q.shape)
    gk = gk_ref[...]
    qg = q * gk
    nope = jnp.where(lane < MLA_NOPE_DIM, qg, 0.0)
    nope = nope + pltpu.roll(nope, MLA_NOPE_DIM, 1)
    qn = jnp.concatenate([nope] * (MLA_HEADS * MLA_NOPE_DIM // LANES), axis=1)
    own = _lane_iota(qn.shape) // MLA_NOPE_DIM == _row_iota(qn.shape) // tq
    q_abs = _dot(jnp.where(own, qn, 0.0).astype(BF16), wknt_ref[...])
    up =pltpu.roll(q, LANES - ROPE_HALF, 1)
    dn = pltpu.roll(q, ROPE_HALF, 1)
    in_x1 = (lane >= MLA_NOPE_DIM) & (lane < MLA_NOPE_DIM + ROPE_HALF)
    in_x2 = (lane >= MLA_NOPE_DIM + ROPE_HALF) & (lane < MLA_QK_DIM)
    q_hat = jnp.where(in_x1, up, jnp.where(in_x2, -dn, 0.0)) * gk
    q_cos = pltpu.roll(jnp.where(lane >= MLA_NOPE_DIM, qg, 0.0), LANES - MLA_NOPE_DIM, 1)
    q_sin = pltpu.roll(q_hat, LANES - MLA_ROPE_DIM, 1)
    ones_rows = jnp.where((_lane_iota((SUBLANES, LANES)) >= 2 * MLA_ROPE_DIM), 1.0, 0.0)
    lhs_r = jnp.concatenate([jnp.where(lane < MLA_ROPE_DIM, q_cos, jnp.where(lane < 2 * MLA_ROPE_DIM, q_sin, 0.0)),
                             ones_rows], axis=0).astype(BF16)
    q_abs = q_abs.astype(BF16)

    m_ref[...] = jnp.full(m_ref.shape, NEG_INF, F32)
    l_ref[...] = jnp.zeros(l_ref.shape, F32)
    acc_ref[...] = jnp.zeros(acc_ref.shape, F32)

    def attend(rows, tab, allowed):
        nk = rows.shape[0]
        c = rows[:, :MLA_KV_RANK].astype(BF16)
        kr = rows[:, MLA_KV_RANK:]
        sq = kr * kr
        hi = sq.astype(BF16).astype(F32)
        rhs = (jnp.concatenate([kr, kr, hi, sq - hi], axis=1) * tab).astype(BF16)
        sr = _dot_t(lhs_r, rhs)
        knt = _dot_t(wknt_ref[...], c)
        inv = []
        for hd in range(MLA_HEADS):
            blk = knt[hd * MLA_NOPE_DIM:(hd + 1) * MLA_NOPE_DIM]
            ms = (jnp.sum(blk * blk, axis=0, keepdims=True) + sr[nrow:nrow + 1]) * (1.0 / MLA_QK_DIM)
            inv.append(jnp.broadcast_to(lax.rsqrt(ms + NORM_EPS), (tq, nk)))
        s = (_dot_t(q_abs, c) + sr[:nrow]) * jnp.concatenate(inv, axis=0)
        if allowed is not None:
            s = jnp.where(allowed, s, NEG_INF)
        m_old = m_ref[...]
        m_new = jnp.maximum(m_old, jnp.max(s, axis=-1, keepdims=True))
        alpha = jnp.exp(m_old - m_new)
        p = jnp.exp(s - m_new)
        l_ref[...] = alpha * l_ref[...] + jnp.sum(p, axis=-1, keepdims=True)
        acc_ref[...] = alpha * acc_ref[...] + _dot(p.astype(BF16), c)
        m_ref[...] = m_new

    causal = _row_iota((tq, tq)) >= _lane_iota((tq, tq))
    attend(new_ref[...], tabn_ref[...], jnp.concatenate([causal] * MLA_HEADS, axis=0))

    def body(j, slot):
        attend(buf_ref[slot], tab_ref[pl.ds(pl.multiple_of(j * kp, kp), kp), :], None)

    _paged_loop(pt_ref, n_chunks, pages, page_copy, body)

    o = acc_ref[...] / l_ref[...]
    outs = [_dot(o[hd * tq:(hd + 1) * tq].astype(BF16), wv_ref[hd]) for hd in range(MLA_HEADS)]
    o_ref[...] = jnp.concatenate(outs, axis=1).astype(o_ref.dtype)


def _mla_key_tables(pos):
    inv = jnp.power(ROPE_BASE, -jnp.arange(ROPE_HALF, dtype=F32) / ROPE_HALF)
    ang = pos.astype(F32)[:, None] * inv[None, :]
    cos, sin = jnp.cos(ang), jnp.sin(ang)
    return jnp.concatenate([cos, cos, sin, sin, jnp.ones((pos.shape[0], LANES - 2 * MLA_ROPE_DIM), F32)], axis=1)


def _mla_sample(page_table, q_rows, cache, rows_new, tab_past, tab_new, wknt, wv, gk):
    n_seq, n_pages = page_table.shape
    tq = rows_new.shape[1]
    nrow = MLA_HEADS * tq
    pages = min(_MLA_PAGES, n_pages)
    n_chunks = n_pages // pages
    kern = functools.partial(_mla_sample_kernel, n_chunks=n_chunks, pages=pages)
    per_seq = lambda shp: pl.BlockSpec((None,) + shp, lambda s, pt: (s, 0, 0))
    cs = lambda c: pl.BlockSpec(c.shape, lambda s, pt: (0,) * c.ndim)
    ow = MLA_HEADS * MLA_V_DIM
    return pl.pallas_call(
        kern,
        grid_spec=_seq_grid_spec(
            n_seq,
            [per_seq((nrow, LANES)), pl.BlockSpec(memory_space=pl.ANY), per_seq((tq, _MLA_ROW_W)),
             cs(tab_past), cs(tab_new), cs(wknt), cs(wv), cs(gk)],
            per_seq((tq, ow)),
            [pltpu.VMEM((2, pages * PAGE_SIZE, _MLA_ROW_W), F32), pltpu.SemaphoreType.DMA((2,)),
             pltpu.VMEM((nrow, 1), F32), pltpu.VMEM((nrow, 1), F32), pltpu.VMEM((nrow, MLA_KV_RANK), F32)]),
        out_shape=jax.ShapeDtypeStruct((n_seq, tq, ow), BF16),
        compiler_params=_cparams(("arbitrary",)),
        name="mla_sample",
    )(page_table, q_rows, cache, rows_new, tab_past, tab_new, wknt, wv, gk)


_MEM_SEQS_PER_STEP = 8


def _token_tile(n):
    return min(_TM, n)


def _after_mixers(x, o_n, o_m, mem_kv, mem_tokens, mem_seqs, lw, past, seq_len):
    tm = _token_tile(x.shape[0])
    x1, qmem = _merge(x, o_n, o_m, lw['g1'], lw['wmg'], lw['wno'], lw['wmo'], lw['wout'],
                      lw['g2'], lw['mem_wq'], lw['mem_qg'], tm)
    om = _mem_attn(qmem, mem_kv, mem_tokens, mem_seqs)
    return _ffn(x1, om, lw['mem_wo'], lw['g3'], lw['wup'], lw['conv_w'], lw['conv_b'], lw['wdn'],
                past, tm, seq_len)


def kernel(x_prompt, x_sample, cache_nsa, cache_mla, state_win, cache_mem, state_conv, page_table,
           mem_prompt, rel_bias, norm1_g, w_in, nsa_q_norm, nsa_k_norm, nsa_cmp_w1, nsa_cmp_b1,
           nsa_cmp_w2, nsa_w_o, mla_q_norm, mla_kv_norm, mla_w_uq, mla_w_ukv, mla_qk_q_norm,
           mla_qk_k_norm, mla_w_o, w_out, norm2_g, mem_norm_g, mem_w_q, mem_w_kv, mem_q_norm,
           mem_k_norm, mem_w_o, norm3_g, ffn_w_up, ffn_conv_w, ffn_conv_b, ffn_w_down):
    depth = w_in.shape[0]
    bsz, t, _ = x_prompt.shape
    n_seq, tq, _ = x_sample.shape
    n_pool = cache_nsa.shape[1]
    n_pages = page_table.shape[1]
    past_len = n_pages * PAGE_SIZE
    w_s = state_win.shape[2]
    mem_len = mem_prompt.shape[1]
    assert t % _TM == 0 and past_len % SEL_BLOCK == 0 and tq <= SEL_BLOCK and tq % SUBLANES == 0
    layer = lambda a, l: a.reshape(a.shape[1:]) if depth == 1 else a[l]

    xp = x_prompt.reshape(bsz * t, D_MODEL)
    xs = x_sample.reshape(n_seq * tq, D_MODEL)
    tm_s = _token_tile(n_seq * tq)
    tabs_p = _rope_tables(jnp.arange(t, dtype=jnp.int32))
    pos_s = past_len + jnp.arange(tq, dtype=jnp.int32)
    tabs_s = tuple(jnp.tile(tb, (tm_s // tq, 1)) for tb in _rope_tables(pos_s))
    key_tab_past = _mla_key_tables(jnp.arange(past_len, dtype=jnp.int32))
    key_tab_new = _mla_key_tables(pos_s)
    cache_nsa_blocks = cache_nsa.reshape(depth * n_pool, _BLK_PER_PAGE, CMP_BLOCK, _ROWS_W)
    cache_nsa_rows = cache_nsa.reshape(depth * n_pool, PAGE_SIZE, _ROWS_W)
    cache_mla_rows = cache_mla.reshape(depth * n_pool, PAGE_SIZE, _MLA_ROW_W)
    mem_flat = mem_prompt.reshape(bsz * mem_len, D_MODEL)

    outs = [[] for _ in range(9)]
    for l in range(depth):
        pw = _proj_weights(w_in[l], nsa_q_norm[l], nsa_k_norm[l], mla_q_norm[l], mla_kv_norm[l],
                           mla_w_uq[l], mla_qk_q_norm[l], mla_w_ukv[l], mla_qk_k_norm[l])
        cw = _compress_weights(nsa_cmp_w1[l], nsa_cmp_b1[l], nsa_cmp_w2[l], nsa_k_norm[l, 0])
        wno = nsa_w_o[l].reshape(NSA_GROUPS, NSA_HPG, NSA_HEAD_DIM, D_MODEL)
        wno = jnp.transpose(wno, (1, 0, 2, 3)).reshape(_QN_W, D_MODEL)
        lw = dict(g1=norm1_g[l], wmg=w_in[l][:, -2 * D_MODEL:].astype(BF16), wno=wno.astype(BF16),
                  wmo=mla_w_o[l].astype(BF16), wout=w_out[l].astype(BF16), g2=norm2_g[l],
                  mem_wq=mem_w_q[l].astype(BF16), mem_qg=mem_q_norm[l], mem_wo=mem_w_o[l].astype(BF16),
                  g3=norm3_g[l], wup=ffn_w_up[l].astype(BF16), conv_w=ffn_conv_w[l], conv_b=ffn_conv_b[l],
                  wdn=ffn_w_down[l].astype(BF16))

        pr = _project(xp, tabs_p, t // _TM, norm1_g[l], pw, _TM)
        nb_p = bsz * t // CMP_BLOCK
        kc_p = _compress_rows(pr['rows'].reshape(nb_p, CMP_BLOCK, _ROWS_W), cw, min(LANES, nb_p))
        per_b = lambda a: a.reshape(bsz, t, a.shape[-1])
        o_n = _nsa_prompt(per_b(pr['qn']), per_b(pr['gate']), kc_p.reshape(bsz, t // CMP_BLOCK, _CMP_W),
                          per_b(pr['rows']), per_b(pr['win']), rel_bias)
        o_m = _mla_prompt(per_b(pr['qm']), per_b(pr['kmla']), per_b(pr['cb']), pw['wv'])
        mkv = _mem_kv(mem_flat, mem_norm_g[l], mem_w_kv[l].astype(BF16), mem_k_norm[l])
        xp, a_tail = _after_mixers(xp, o_n.reshape(bsz * t, -1), o_m.reshape(bsz * t, -1),
                                   mkv.reshape(bsz, mem_len, 2 * _MEM_W), _TM, 1, lw, None, t)
        outs[0].append(pr['rows'].reshape(bsz, t, 4, NSA_GROUPS, NSA_HEAD_DIM))
        outs[1].append(pr['mrow'].reshape(bsz, t, _MLA_ROW_W))
        outs[2].append(pr['win'].reshape(bsz, t, 2, NSA_GROUPS, NSA_HEAD_DIM)[:, -min(WINDOW, t):])
        outs[3].append(mkv.reshape(bsz, mem_len, 2, MEM_HEADS, MEM_HEAD_DIM))
        outs[4].append(a_tail.reshape(bsz, t // _TM, _CARRY_ROWS, D_FF)[:, -1, _CARRY_ROWS - (CONV_W - 1):])

        sr = _project(xs, tabs_s, 1, norm1_g[l], pw, tm_s)
        rows_s = sr['rows'].reshape(n_seq, tq, _ROWS_W)
        win_new = sr['win'].reshape(n_seq, tq, _WIN_W)
        pt_l = page_table + l * n_pool
        tail = jnp.pad(rows_s, ((0, 0), (0, _TAIL_BLOCKS * CMP_BLOCK - tq), (0, 0)))
        kc_s = _compress_paged(pt_l, cache_nsa_blocks, tail.reshape(n_seq, _TAIL_BLOCKS, CMP_BLOCK, _ROWS_W), cw)
        win_old = layer(state_win, l).reshape(n_seq, w_s, _WIN_W)
        o_n = _nsa_sample(pt_l, rel_bias, sr['qn'].reshape(n_seq, tq, _QN_W), sr['gate'].reshape(n_seq, tq, LANES),
                          kc_s, cache_nsa_rows, rows_s, win_old, win_new, past_len)
        q_rows = jnp.transpose(sr['qm'].reshape(n_seq, tq, MLA_HEADS, MLA_HEAD_PAD), (0, 2, 1, 3))
        mrow_s = sr['mrow'].reshape(n_seq, tq, _MLA_ROW_W)
        o_m = _mla_sample(pt_l, q_rows.reshape(n_seq, MLA_HEADS * tq, MLA_HEAD_PAD), cache_mla_rows,
                          mrow_s, key_tab_past, key_tab_new, pw['wkn_flat'].T, pw['wv'], pw['qkk'])
        past = jnp.pad(layer(state_conv, l), ((0, 0), (0, tq - (CONV_W - 1)), (0, 0)))
        xs, a_s = _after_mixers(xs, o_n.reshape(n_seq * tq, -1), o_m.reshape(n_seq * tq, -1),
                                layer(cache_mem, l).reshape(n_seq, mem_len, 2 * _MEM_W), tq,
                                min(_MEM_SEQS_PER_STEP, n_seq), lw, past.reshape(n_seq * tq, D_FF), tq)
        outs[5].append(rows_s.reshape(n_seq, tq, 4, NSA_GROUPS, NSA_HEAD_DIM))
        outs[6].append(mrow_s)
        win_all = jnp.concatenate([win_old, win_new], axis=1)[:, -w_s:]
        outs[7].append(win_all.reshape(n_seq, w_s, 2, NSA_GROUPS, NSA_HEAD_DIM))
        outs[8].append(a_s.reshape(n_seq, tq, D_FF)[:, tq - (CONV_W - 1):])

    return (xp.reshape(bsz, t, D_MODEL), xs.reshape(n_seq, tq, D_MODEL)) + tuple(jnp.stack(o) for o in outs)
```

```python
import functools
import math

import numpy as np
import jax
import jax.numpy as jnp
from jax import lax
from jax.experimental import pallas as pl
from jax.experimental.pallas import tpu as pltpu

D_MODEL = 1024
PAGE_SIZE = 128
NSA_HEADS = 8
NSA_GROUPS = 2
NSA_HPG = NSA_HEADS // NSA_GROUPS
NSA_HEAD_DIM = 64
CMP_BLOCK = 32
SEL_BLOCK = 64
TOP_N = 16
WINDOW = 512
MLA_HEADS = 8
MLA_Q_RANK = 384
MLA_KV_RANK = 256
MLA_NOPE_DIM = 64
MLA_ROPE_DIM = 32
MLA_QK_DIM = MLA_NOPE_DIM + MLA_ROPE_DIM
MLA_V_DIM = 64
ROPE_BASE = 10000.0
MEM_HEADS = 4
MEM_HEAD_DIM = 128
D_FF = 2816
CONV_W = 3
REL_BUCKETS = 32
REL_MAX_DIST = 128
NORM_EPS = 1e-6
NEG_INF = -1e30
FORCED_SCORE = 1e9

LANES = 128
SUBLANES = 8
VMEM_LIMIT = 56 * 1024 * 1024

HALF = LANES // 2
MLA_HEAD_PAD = LANES
ROPE_HALF = MLA_ROPE_DIM // 2

F32 = jnp.float32
BF16 = jnp.bfloat16


def _cparams(sem):
    return pltpu.CompilerParams(dimension_semantics=sem, vmem_limit_bytes=VMEM_LIMIT)


def _const_spec(shape):
    nd = len(shape)
    return pl.BlockSpec(shape, lambda *_: (0,) * nd)


def _dot(a, b):
    return jnp.dot(a, b, preferred_element_type=F32)


def _dot_t(a, b):
    return lax.dot_general(a, b, (((1,), (1,)), ((), ())), preferred_element_type=F32)


def _lane_iota(shape):
    return lax.broadcasted_iota(jnp.int32, shape, len(shape) - 1)


def _row_iota(shape):
    return lax.broadcasted_iota(jnp.int32, shape, len(shape) - 2)


def _rms_rows(x, gain):
    ms = jnp.mean(x * x, axis=-1, keepdims=True)
    return x * lax.rsqrt(ms + NORM_EPS) * gain


def _rms_half_lanes(x, gain):
    low = _lane_iota(x.shape) < HALF
    x2 = x * x
    s_lo = jnp.sum(jnp.where(low, x2, 0.0), axis=-1, keepdims=True)
    s_hi = jnp.sum(jnp.where(low, 0.0, x2), axis=-1, keepdims=True)
    ms = jnp.where(low, s_lo, s_hi) * (1.0 / HALF)
    return x * lax.rsqrt(ms + NORM_EPS) * gain


def _rope_block(x, c, s1, s2):
    up = pltpu.roll(x, LANES - ROPE_HALF, 1)
    dn = pltpu.roll(x, ROPE_HALF, 1)
    return x * c + up * s1 + dn * s2


def _rel_bucket_upper():
    exact = REL_BUCKETS // 2
    ub = []
    for b in range(REL_BUCKETS - 1):
        if b < exact:
            ub.append(b)
        else:
            d = b
            while True:
                nxt = d + 1
                lg = exact + int(math.log(nxt / exact) / math.log(REL_MAX_DIST / exact) * (REL_BUCKETS - exact))
                if min(lg, REL_BUCKETS - 1) > b:
                    break
                d = nxt
            ub.append(d)
    return ub


_BUCKET_UB = _rel_bucket_upper()


def _bias_from_dist(dist, rb_ref, head):
    out = jnp.full(dist.shape, rb_ref[REL_BUCKETS - 1, head], F32)
    for b in range(REL_BUCKETS - 2, -1, -1):
        out = jnp.where(dist <= _BUCKET_UB[b], rb_ref[b, head], out)
    return out


_QN_W = NSA_HEADS * NSA_HEAD_DIM
_KV_W = 6 * NSA_GROUPS * NSA_HEAD_DIM
_PROJ_W = _QN_W + _KV_W + MLA_Q_RANK + MLA_KV_RANK + 2 * LANES
_O_KV = _QN_W
_O_CQ = _O_KV + _KV_W
_O_CKV = _O_CQ + MLA_Q_RANK
_O_KR = _O_CKV + MLA_KV_RANK
_O_G = _O_KR + LANES
_MLA_PAD_W = MLA_HEADS * MLA_HEAD_PAD
_ROWS_W = 4 * NSA_GROUPS * NSA_HEAD_DIM
_WIN_W = 2 * NSA_GROUPS * NSA_HEAD_DIM
_MLA_ROW_W = MLA_KV_RANK + MLA_ROPE_DIM


def _proj_kernel(x_ref, tc_ref, ts1_ref, ts2_ref, g1_ref, w_ref, qg_ref, ksg_ref, kwg_ref,
                 cqg_ref, ckvg_ref, wuq_ref, qkq_ref, wkn_ref, qkk_ref,
                 qn_ref, rows_ref, win_ref, gate_ref, qm_ref, mrow_ref, kmla_ref, cb_ref):
    x = x_ref[...]
    h = _rms_rows(x, g1_ref[...]).astype(BF16)
    y = _dot(h, w_ref[...])

    nsa_scale = NSA_HEAD_DIM ** -0.5
    for v in range(_QN_W // LANES):
        blk = y[:, v * LANES:(v + 1) * LANES]
        qn_ref[:, v * LANES:(v + 1) * LANES] = (_rms_half_lanes(blk, qg_ref[...]) * nsa_scale).astype(BF16)

    kv = lambda slot: y[:, _O_KV + slot * LANES:_O_KV + (slot + 1) * LANES]
    rows_ref[:, 0:LANES] = kv(0)
    rows_ref[:, LANES:2 * LANES] = kv(1)
    rows_ref[:, 2 * LANES:3 * LANES] = _rms_half_lanes(kv(2), ksg_ref[...])
    rows_ref[:, 3 * LANES:4 * LANES] = kv(3)
    win_ref[:, 0:LANES] = _rms_half_lanes(kv(4), kwg_ref[...])
    win_ref[:, LANES:2 * LANES] = kv(5)
    gate_ref[...] = jax.nn.sigmoid(y[:, _O_G:_O_G + LANES])

    tc, ts1, ts2 = tc_ref[...], ts1_ref[...], ts2_ref[...]
    mla_scale = MLA_QK_DIM ** -0.5
    inv_qk = 1.0 / MLA_QK_DIM

    cq = _rms_rows(y[:, _O_CQ:_O_CQ + MLA_Q_RANK], cqg_ref[...]).astype(BF16)
    qraw = _dot(cq, wuq_ref[...])
    for hd in range(MLA_HEADS):
        blk = qraw[:, hd * LANES:(hd + 1) * LANES]
        ms = jnp.sum(blk * blk, axis=-1, keepdims=True) * inv_qk
        blk = blk * lax.rsqrt(ms + NORM_EPS) * qkq_ref[...]
        qm_ref[:, hd * LANES:(hd + 1) * LANES] = (_rope_block(blk, tc, ts1, ts2) * mla_scale).astype(BF16)

    c = _rms_rows(y[:, _O_CKV:_O_CKV + MLA_KV_RANK], ckvg_ref[...])
    krv = y[:, _O_KR:_O_KR + LANES]
    mrow_ref[:, 0:MLA_KV_RANK] = c
    mrow_ref[:, MLA_KV_RANK:_MLA_ROW_W] = krv[:, 0:MLA_ROPE_DIM]
    cbf = c.astype(BF16)
    cb_ref[...] = cbf

    kr_at = pltpu.roll(jnp.where(_lane_iota(krv.shape) < MLA_ROPE_DIM, krv, 0.0), MLA_NOPE_DIM, 1)
    ss_r = jnp.sum(kr_at * kr_at, axis=-1, keepdims=True)
    kr_rot = _rope_block(kr_at * qkk_ref[...], tc, ts1, ts2)
    kn = _dot(cbf, wkn_ref[...])
    for hd in range(MLA_HEADS):
        blk = kn[:, hd * LANES:(hd + 1) * LANES]
        ms = (jnp.sum(blk * blk, axis=-1, keepdims=True) + ss_r) * inv_qk
        kmla_ref[:, hd * LANES:(hd + 1) * LANES] = (
            (blk * qkk_ref[...] + kr_rot) * lax.rsqrt(ms + NORM_EPS)).astype(BF16)


def _rope_tables(pos):
    inv = jnp.power(ROPE_BASE, -jnp.arange(ROPE_HALF, dtype=F32) / ROPE_HALF)
    ang = pos.astype(F32)[:, None] * inv[None, :]
    cos, sin = jnp.cos(ang), jnp.sin(ang)
    n = pos.shape[0]
    z = lambda w: jnp.zeros((n, w), F32)
    pad = MLA_HEAD_PAD - MLA_QK_DIM
    tc = jnp.concatenate([jnp.ones((n, MLA_NOPE_DIM), F32), cos, cos, z(pad)], axis=1)
    ts1 = jnp.concatenate([z(MLA_NOPE_DIM), -sin, z(ROPE_HALF), z(pad)], axis=1)
    ts2 = jnp.concatenate([z(MLA_NOPE_DIM), z(ROPE_HALF), sin, z(pad)], axis=1)
    return tc, ts1, ts2


def _pad_heads(w, n_heads, width, used):
    k = w.shape[0]
    w = w.reshape(k, n_heads, width)[:, :, :used]
    return jnp.pad(w, ((0, 0), (0, 0), (0, MLA_HEAD_PAD - used))).reshape(k, n_heads * MLA_HEAD_PAD)


def _mla_gain_pad(g):
    return jnp.pad(g, (0, MLA_HEAD_PAD - MLA_QK_DIM)).reshape(1, MLA_HEAD_PAD)


def _proj_weights(w_in, nsa_q_norm, nsa_k_norm, mla_q_norm, mla_kv_norm, mla_w_uq, mla_qk_q_norm,
                  mla_w_ukv, mla_qk_k_norm):
    o = np.cumsum((0, _QN_W, _KV_W, 3 * NSA_HEADS, MLA_Q_RANK, MLA_KV_RANK, MLA_ROPE_DIM))
    qn = w_in[:, o[0]:o[1]].reshape(D_MODEL, NSA_GROUPS, NSA_HPG, NSA_HEAD_DIM)
    qn = jnp.transpose(qn, (0, 2, 1, 3)).reshape(D_MODEL, _QN_W)
    padc = lambda w: jnp.pad(w, ((0, 0), (0, LANES - w.shape[1])))
    w = jnp.concatenate([qn, w_in[:, o[1]:o[2]], w_in[:, o[3]:o[4]], w_in[:, o[4]:o[5]],
                         padc(w_in[:, o[5]:o[6]]), padc(w_in[:, o[2]:o[3]])], axis=1).astype(BF16)
    tile2 = lambda g: jnp.tile(g, NSA_GROUPS).reshape(1, LANES)
    wkn = mla_w_ukv.reshape(MLA_KV_RANK, MLA_HEADS, MLA_NOPE_DIM + MLA_V_DIM)
    wv = jnp.transpose(wkn[:, :, MLA_NOPE_DIM:], (1, 0, 2)).astype(BF16)
    wkn_flat = wkn[:, :, :MLA_NOPE_DIM].reshape(MLA_KV_RANK, MLA_HEADS * MLA_NOPE_DIM)
    return dict(
        w=w, qg=tile2(nsa_q_norm), ksg=tile2(nsa_k_norm[1]), kwg=tile2(nsa_k_norm[2]),
        cqg=mla_q_norm.reshape(1, -1), ckvg=mla_kv_norm.reshape(1, -1),
        wuq=_pad_heads(mla_w_uq, MLA_HEADS, MLA_QK_DIM, MLA_QK_DIM).astype(BF16),
        qkq=_mla_gain_pad(mla_qk_q_norm),
        wkn=_pad_heads(wkn_flat, MLA_HEADS, MLA_NOPE_DIM, MLA_NOPE_DIM).astype(BF16),
        qkk=_mla_gain_pad(mla_qk_k_norm), wv=wv, wkn_flat=wkn_flat.astype(BF16))


def _project(x, pos_tables, n_pos_tiles, g1, pw, tm):
    n = x.shape[0]
    grid = (n // tm,)
    row = lambda w: pl.BlockSpec((tm, w), lambda i: (i, 0))
    tab = pl.BlockSpec((tm, LANES), lambda i: (i % n_pos_tiles, 0))
    consts = [g1.reshape(1, -1), pw['w'], pw['qg'], pw['ksg'], pw['kwg'], pw['cqg'], pw['ckvg'],
              pw['wuq'], pw['qkq'], pw['wkn'], pw['qkk']]
    outs = pl.pallas_call(
        _proj_kernel,
        grid=grid,
        in_specs=[row(D_MODEL), tab, tab, tab] + [_const_spec(c.shape) for c in consts],
        out_specs=[row(_QN_W), row(_ROWS_W), row(_WIN_W), row(LANES), row(_MLA_PAD_W),
                   row(_MLA_ROW_W), row(_MLA_PAD_W), row(MLA_KV_RANK)],
        out_shape=[jax.ShapeDtypeStruct((n, _QN_W), BF16), jax.ShapeDtypeStruct((n, _ROWS_W), F32),
                   jax.ShapeDtypeStruct((n, _WIN_W), F32), jax.ShapeDtypeStruct((n, LANES), F32),
                   jax.ShapeDtypeStruct((n, _MLA_PAD_W), BF16), jax.ShapeDtypeStruct((n, _MLA_ROW_W), F32),
                   jax.ShapeDtypeStruct((n, _MLA_PAD_W), BF16), jax.ShapeDtypeStruct((n, MLA_KV_RANK), BF16)],
        compiler_params=_cparams(("arbitrary",)),
        name="proj",
    )(x, *pos_tables, *consts)
    return dict(zip(('qn', 'rows', 'win', 'gate', 'qm', 'mrow', 'kmla', 'cb'), outs))


_CMP_W = 2 * NSA_GROUPS * NSA_HEAD_DIM


def _compress_math(x_ref, w1_ref, b1_ref, w2_ref, kg_ref, row_major=True):
    rows_at = (lambda r: x_ref[:, r, :]) if row_major else (lambda r: x_ref[r])
    acc = _dot(rows_at(0).astype(BF16), w1_ref[0])
    for r in range(1, CMP_BLOCK):
        acc = acc + _dot(rows_at(r).astype(BF16), w1_ref[r])
    y = acc + b1_ref[...]
    y = y * jax.nn.sigmoid(y)
    z = _dot(y.astype(BF16), w2_ref[...])
    return jnp.concatenate([_rms_half_lanes(z[:, :LANES], kg_ref[...]), z[:, LANES:]], axis=1)


def _compress_kernel(x_ref, w1_ref, b1_ref, w2_ref, kg_ref, o_ref):
    o_ref[...] = _compress_math(x_ref, w1_ref, b1_ref, w2_ref, kg_ref)


def _compress_weights(w1, b1, w2, kc_norm):
    eye = jnp.eye(NSA_GROUPS, dtype=F32)
    w1r = w1.reshape(2, CMP_BLOCK, NSA_HEAD_DIM, NSA_HEAD_DIM)
    w1e = jnp.einsum('crde,cx,gy->rcgdxye', w1r, jnp.eye(2, dtype=F32), eye)
    w1e = w1e.reshape(CMP_BLOCK, _CMP_W, _CMP_W).astype(BF16)
    w2e = jnp.einsum('cde,cx,gy->cgdxye', w2, jnp.eye(2, dtype=F32), eye).reshape(_CMP_W, _CMP_W).astype(BF16)
    b1e = jnp.broadcast_to(b1[:, None, :], (2, NSA_GROUPS, NSA_HEAD_DIM)).reshape(1, _CMP_W)
    return dict(w1=w1e, b1=b1e, w2=w2e, kg=jnp.tile(kc_norm, NSA_GROUPS).reshape(1, LANES))


def _compress_rows(rows, cw, nbt):
    nb = rows.shape[0]
    consts = [cw['w1'], cw['b1'], cw['w2'], cw['kg']]
    return pl.pallas_call(
        _compress_kernel,
        grid=(nb // nbt,),
        in_specs=[pl.BlockSpec((nbt, CMP_BLOCK, _CMP_W), lambda i: (i, 0, 0))]
        + [_const_spec(c.shape) for c in consts],
        out_specs=pl.BlockSpec((nbt, _CMP_W), lambda i: (i, 0)),
        out_shape=jax.ShapeDtypeStruct((nb, _CMP_W), F32),
        compiler_params=_cparams(("arbitrary",)),
        name="compress",
    )(rows, *consts)


_TQ = 128
_NROW = NSA_HEADS * _TQ


def _stack_heads(q):
    low = _lane_iota((q.shape[0], LANES)) < HALF
    zero = jnp.zeros((), q.dtype)
    blocks = [q[:, v * LANES:(v + 1) * LANES] for v in range(NSA_HPG)]
    return jnp.concatenate([jnp.where(low, b, zero) for b in blocks]
                           + [jnp.where(low, zero, b) for b in blocks], axis=0)


def _unstack_heads(o, tq):
    low = _lane_iota((tq, LANES)) < HALF
    return jnp.concatenate(
        [jnp.where(low, o[v * tq:(v + 1) * tq], o[(NSA_HPG + v) * tq:(NSA_HPG + v + 1) * tq])
         for v in range(NSA_HPG)], axis=1)


def _exact_dot01(x, m01):
    a = x.astype(BF16)
    r1 = x - a.astype(F32)
    b = r1.astype(BF16)
    c = (r1 - b.astype(F32)).astype(BF16)
    return _dot(a, m01) + _dot(b, m01) + _dot(c, m01)


def _top_blocks(imp, blk, n_pick):
    big = imp.shape[-1]
    sel = jnp.zeros(imp.shape, F32)
    x = imp
    for _ in range(n_pick):
        m = jnp.max(x, axis=-1, keepdims=True)
        idx = jnp.min(jnp.where(x == m, blk, big), axis=-1, keepdims=True)
        hit = blk == idx
        sel = jnp.where(hit, 1.0, sel)
        x = jnp.where(hit, -3e38, x)
    return sel


def _flash_step(s, allowed, v, m_ref, l_ref, acc_ref, v_feature_major=False):
    s = jnp.where(allowed, s, NEG_INF)
    m_old = m_ref[...]
    m_new = jnp.maximum(m_old, jnp.max(s, axis=-1, keepdims=True))
    alpha = jnp.exp(m_old - m_new)
    p = jnp.exp(s - m_new)
    l_ref[...] = alpha * l_ref[...] + jnp.sum(p, axis=-1, keepdims=True)
    pb = p.astype(BF16)
    acc_ref[...] = alpha * acc_ref[...] + (_dot_t(pb, v) if v_feature_major else _dot(pb, v))
    m_ref[...] = m_new


def _nsa_prompt_kernel(rb_ref, q_ref, gate_ref, kc_ref, sel_ref, win_ref, o_ref,
                       bias_ref, ms_ref, ls_ref, as_ref, mw_ref, lw_ref, aw_ref, *, n_sel, n_pick):
    b = pl.program_id(0)
    qi = pl.program_id(1)
    tq = _TQ
    t_loc = _row_iota((tq, LANES))
    k_loc = _lane_iota((tq, LANES))

    @pl.when((b == 0) & (qi == 0))
    def _():
        for kind in range(3):
            dist = t_loc - k_loc + kind * tq
            for hd in range(NSA_HEADS):
                bias_ref[kind, hd * tq:(hd + 1) * tq, :] = _bias_from_dist(dist, rb_ref, hd)

    q8 = _stack_heads(q_ref[...])
    t_pos = qi * tq + t_loc

    kc = kc_ref[...]
    n_cmp = kc.shape[0]
    s_c = _dot_t(q8, kc[:, :LANES].astype(BF16))
    t_pos_c = qi * tq + _row_iota((tq, n_cmp))
    dist_c = t_pos_c - (_lane_iota((tq, n_cmp)) * CMP_BLOCK + (CMP_BLOCK - 1))
    valid_c = jnp.concatenate([dist_c >= 0] * NSA_HEADS, axis=0)
    bias_c = jnp.concatenate([_bias_from_dist(dist_c, rb_ref, hd) for hd in range(NSA_HEADS)], axis=0)
    s_c = jnp.where(valid_c, s_c + bias_c, NEG_INF)
    e_c = jnp.exp(s_c - jnp.max(s_c, axis=-1, keepdims=True))
    p_c = jnp.where(valid_c, e_c / jnp.sum(e_c, axis=-1, keepdims=True), 0.0)
    o_c = _dot(p_c.astype(BF16), kc[:, LANES:].astype(BF16))

    pair = (_row_iota((n_cmp, n_sel)) // (SEL_BLOCK // CMP_BLOCK) == _lane_iota((n_cmp, n_sel)))
    pair = jnp.where(pair, 1.0, 0.0).astype(BF16)
    imp_n = []
    for g in range(NSA_GROUPS):
        acc_g = p_c[(g * NSA_HPG) * tq:(g * NSA_HPG + 1) * tq]
        for hl in range(1, NSA_HPG):
            acc_g = acc_g + p_c[(g * NSA_HPG + hl) * tq:(g * NSA_HPG + hl + 1) * tq]
        imp_n.append(acc_g)
    imp = _exact_dot01(jnp.concatenate(imp_n, axis=0), pair)
    blk = _lane_iota(imp.shape)
    cur = (qi * tq + _row_iota(imp.shape) % tq) // SEL_BLOCK
    forced = (blk == 0) | (blk == cur) | (blk == cur - 1)
    imp = jnp.where(forced, FORCED_SCORE, jnp.where(blk <= cur, imp, -1.0))
    sel_all = _top_blocks(imp, blk, n_pick).astype(BF16)
    sels = [sel_all[g * tq:(g + 1) * tq] for g in range(NSA_GROUPS)]

    for m_ref, l_ref, a_ref in ((ms_ref, ls_ref, as_ref), (mw_ref, lw_ref, aw_ref)):
        m_ref[...] = jnp.full(m_ref.shape, NEG_INF, F32)
        l_ref[...] = jnp.zeros(l_ref.shape, F32)
        a_ref[...] = jnp.zeros(a_ref.shape, F32)

    def tile_common(kt, src_ref):
        rows = src_ref[pl.ds(pl.multiple_of(kt * tq, tq), tq), :]
        kind = jnp.minimum(qi - kt, 2)
        s = _dot_t(q8, rows[:, :LANES].astype(BF16)) + bias_ref[kind]
        dist = t_pos - (kt * tq + k_loc)
        return s, rows[:, LANES:].astype(BF16), dist

    def sel_tile(kt, carry):
        s, v, dist = tile_common(kt, sel_ref)
        expand = (_row_iota((n_sel, tq)) == kt * (tq // SEL_BLOCK) + _lane_iota((n_sel, tq)) // SEL_BLOCK)
        expand = jnp.where(expand, 1.0, 0.0).astype(BF16)
        ok = [(_dot(sels[g], expand) > 0.5) & (dist >= 0) for g in range(NSA_GROUPS)]
        allowed = jnp.concatenate([ok[0]] * NSA_HPG + [ok[1]] * NSA_HPG, axis=0)
        _flash_step(s, allowed, v, ms_ref, ls_ref, as_ref)
        return carry

    def win_tile(kt, carry):
        s, v, dist = tile_common(kt, win_ref)
        ok = (dist >= 0) & (dist < WINDOW)
        _flash_step(s, jnp.concatenate([ok] * NSA_HEADS, axis=0), v, mw_ref, lw_ref, aw_ref)
        return carry

    lax.fori_loop(0, qi + 1, sel_tile, 0)
    lax.fori_loop(jnp.maximum(qi - WINDOW // tq, 0), qi + 1, win_tile, 0)

    o_s = as_ref[...] / ls_ref[...]
    o_w = aw_ref[...] / lw_ref[...]
    gate = gate_ref[...]
    outs = []
    for hd in range(NSA_HEADS):
        sl = slice(hd * tq, (hd + 1) * tq)
        outs.append(gate[:, 3 * hd:3 * hd + 1] * o_c[sl] + gate[:, 3 * hd + 1:3 * hd + 2] * o_s[sl]
                    + gate[:, 3 * hd + 2:3 * hd + 3] * o_w[sl])
    o_ref[...] = _unstack_heads(jnp.concatenate(outs, axis=0), tq).astype(o_ref.dtype)


def _nsa_prompt(qn, gate, kc, rows, win, rel_bias):
    bsz, t, _ = qn.shape
    n_cmp = t // CMP_BLOCK
    n_sel = t // SEL_BLOCK
    kern = functools.partial(_nsa_prompt_kernel, n_sel=n_sel, n_pick=min(TOP_N, n_sel))
    stat = lambda w: pltpu.VMEM((_NROW, w), F32)
    return pl.pallas_call(
        kern,
        grid=(bsz, t // _TQ),
        in_specs=[pl.BlockSpec(memory_space=pltpu.SMEM),
                  pl.BlockSpec((None, _TQ, _QN_W), lambda b, i: (b, i, 0)),
                  pl.BlockSpec((None, _TQ, LANES), lambda b, i: (b, i, 0)),
                  pl.BlockSpec((None, n_cmp, _CMP_W), lambda b, i: (b, 0, 0)),
                  pl.BlockSpec((None, t, _WIN_W), lambda b, i: (b, 0, 1)),
                  pl.BlockSpec((None, t, _WIN_W), lambda b, i: (b, 0, 0))],
        out_specs=pl.BlockSpec((None, _TQ, _QN_W), lambda b, i: (b, i, 0)),
        out_shape=jax.ShapeDtypeStruct((bsz, t, _QN_W), BF16),
        scratch_shapes=[pltpu.VMEM((3, _NROW, LANES), F32),
                        stat(1), stat(1), stat(LANES), stat(1), stat(1), stat(LANES)],
        compiler_params=_cparams(("arbitrary", "arbitrary")),
        name="nsa_prompt",
    )(rel_bias, qn, gate, kc, rows, win)


_TM = 256
_MLA_TK = 512


def _mla_prompt_kernel(q_ref, k_ref, c_ref, wv_ref, o_ref, m_ref, l_ref, acc_ref):
    qi = pl.program_id(1)
    tq, tk = _TM, _MLA_TK
    m_ref[...] = jnp.full(m_ref.shape, NEG_INF, F32)
    l_ref[...] = jnp.zeros(l_ref.shape, F32)
    acc_ref[...] = jnp.zeros(acc_ref.shape, F32)
    t_loc = _row_iota((tq, tk))
    k_loc = _lane_iota((tq, tk))

    def tile(kt, carry):
        start = pl.multiple_of(kt * tk, tk)
        c = c_ref[pl.ds(start, tk), :]
        allowed = qi * tq - kt * tk + t_loc >= k_loc
        for hd in range(MLA_HEADS):
            lanes = slice(hd * LANES, (hd + 1) * LANES)
            s = _dot_t(q_ref[:, lanes], k_ref[pl.ds(start, tk), lanes])
            s = jnp.where(allowed, s, NEG_INF)
            m_old = m_ref[hd]
            m_new = jnp.maximum(m_old, jnp.max(s, axis=-1, keepdims=True))
            alpha = jnp.exp(m_old - m_new)
            p = jnp.exp(s - m_new)
            l_ref[hd] = alpha * l_ref[hd] + jnp.sum(p, axis=-1, keepdims=True)
            acc_ref[hd] = alpha * acc_ref[hd] + _dot(p.astype(BF16), c)
            m_ref[hd] = m_new
        return carry

    lax.fori_loop(0, (qi * tq + tq + tk - 1) // tk, tile, 0)
    outs = [_dot((acc_ref[hd] / l_ref[hd]).astype(BF16), wv_ref[hd]) for hd in range(MLA_HEADS)]
    o_ref[...] = jnp.concatenate(outs, axis=1).astype(o_ref.dtype)


def _mla_prompt(qm, kmla, cb, wv):
    bsz, t, _ = qm.shape
    ow = MLA_HEADS * MLA_V_DIM
    return pl.pallas_call(
        _mla_prompt_kernel,
        grid=(bsz, t // _TM),
        in_specs=[pl.BlockSpec((None, _TM, _MLA_PAD_W), lambda b, i: (b, i, 0)),
                  pl.BlockSpec((None, t, _MLA_PAD_W), lambda b, i: (b, 0, 0)),
                  pl.BlockSpec((None, t, MLA_KV_RANK), lambda b, i: (b, 0, 0)),
                  _const_spec(wv.shape)],
        out_specs=pl.BlockSpec((None, _TM, ow), lambda b, i: (b, i, 0)),
        out_shape=jax.ShapeDtypeStruct((bsz, t, ow), BF16),
        scratch_shapes=[pltpu.VMEM((MLA_HEADS, _TM, 1), F32), pltpu.VMEM((MLA_HEADS, _TM, 1), F32),
                        pltpu.VMEM((MLA_HEADS, _TM, MLA_KV_RANK), F32)],
        compiler_params=_cparams(("arbitrary", "arbitrary")),
        name="mla_prompt",
    )(qm, kmla, cb, wv)


_MEM_W = MEM_HEADS * MEM_HEAD_DIM


def _merge_kernel(x_ref, on_ref, om_ref, g1_ref, wmg_ref, wno_ref, wmo_ref, wout_ref,
                  g2_ref, wq_ref, qg_ref, x1_ref, qmem_ref):
    x = x_ref[...]
    h = _rms_rows(x, g1_ref[...]).astype(BF16)
    mg = jax.nn.sigmoid(_dot(h, wmg_ref[...]))
    a = _dot(on_ref[...], wno_ref[...])
    b = _dot(om_ref[...], wmo_ref[...])
    r = mg[:, :D_MODEL] * a + mg[:, D_MODEL:] * b
    x1 = x + _dot(r.astype(BF16), wout_ref[...])
    x1_ref[...] = x1
    q = _dot(_rms_rows(x1, g2_ref[...]).astype(BF16), wq_ref[...])
    scale = MEM_HEAD_DIM ** -0.5
    for hd in range(MEM_HEADS):
        blk = q[:, hd * LANES:(hd + 1) * LANES]
        qmem_ref[:, hd * LANES:(hd + 1) * LANES] = (_rms_rows(blk, qg_ref[...]) * scale).astype(BF16)


def _merge(x, o_n, o_m, g1, wmg, wno, wmo, wout, g2, wq, qg, tm):
    n = x.shape[0]
    row = lambda w: pl.BlockSpec((tm, w), lambda i: (i, 0))
    consts = [g1.reshape(1, -1), wmg, wno, wmo, wout, g2.reshape(1, -1), wq, qg.reshape(1, -1)]
    return pl.pallas_call(
        _merge_kernel,
        grid=(n // tm,),
        in_specs=[row(D_MODEL), row(_QN_W), row(MLA_HEADS * MLA_V_DIM)] + [_const_spec(c.shape) for c in consts],
        out_specs=[row(D_MODEL), row(_MEM_W)],
        out_shape=[jax.ShapeDtypeStruct((n, D_MODEL), F32), jax.ShapeDtypeStruct((n, _MEM_W), BF16)],
        compiler_params=_cparams(("arbitrary",)),
        name="merge",
    )(x, o_n, o_m, *consts)


def _mem_kv_kernel(m_ref, g_ref, w_ref, kg_ref, o_ref):
    kv = _dot(_rms_rows(m_ref[...], g_ref[...]).astype(BF16), w_ref[...])
    for hd in range(MEM_HEADS):
        o_ref[:, hd * LANES:(hd + 1) * LANES] = _rms_rows(kv[:, hd * LANES:(hd + 1) * LANES], kg_ref[...])
    o_ref[:, _MEM_W:] = kv[:, _MEM_W:]


def _mem_kv(mem, g, w, kg):
    n = mem.shape[0]
    tm = min(n, _TM)
    consts = [g.reshape(1, -1), w, kg.reshape(1, -1)]
    return pl.pallas_call(
        _mem_kv_kernel,
        grid=(n // tm,),
        in_specs=[pl.BlockSpec((tm, D_MODEL), lambda i: (i, 0))] + [_const_spec(c.shape) for c in consts],
        out_specs=pl.BlockSpec((tm, 2 * _MEM_W), lambda i: (i, 0)),
        out_shape=jax.ShapeDtypeStruct((n, 2 * _MEM_W), F32),
        compiler_params=_cparams(("arbitrary",)),
        name="mem_kv",
    )(mem, *consts)


def _mem_attn_kernel(q_ref, kv_ref, o_ref, *, n_seq):
    tq = q_ref.shape[0] // n_seq
    for s in range(n_seq):
        outs = []
        for hd in range(MEM_HEADS):
            q = q_ref[s * tq:(s + 1) * tq, hd * LANES:(hd + 1) * LANES]
            sc = _dot_t(q, kv_ref[s, :, hd * LANES:(hd + 1) * LANES].astype(BF16))
            e = jnp.exp(sc - jnp.max(sc, axis=-1, keepdims=True))
            p = e / jnp.sum(e, axis=-1, keepdims=True)
            outs.append(_dot(p.astype(BF16), kv_ref[s, :, _MEM_W + hd * LANES:_MEM_W + (hd + 1) * LANES].astype(BF16)))
        o_ref[s * tq:(s + 1) * tq, :] = jnp.concatenate(outs, axis=1).astype(o_ref.dtype)


def _mem_attn(q, kv, tq, n_seq):
    n = q.shape[0]
    per_kv = n // kv.shape[0]
    rows = tq * n_seq
    if n_seq == 1:
        kv_map = lambda i: (i * tq // per_kv, 0, 0)
    else:
        kv_map = lambda i: (i, 0, 0)
    return pl.pallas_call(
        functools.partial(_mem_attn_kernel, n_seq=n_seq),
        grid=(n // rows,),
        in_specs=[pl.BlockSpec((rows, _MEM_W), lambda i: (i, 0)),
                  pl.BlockSpec((n_seq,) + kv.shape[1:], kv_map)],
        out_specs=pl.BlockSpec((rows, _MEM_W), lambda i: (i, 0)),
        out_shape=jax.ShapeDtypeStruct((n, _MEM_W), BF16),
        compiler_params=_cparams(("arbitrary",)),
        name="mem_attn",
    )(q, kv)


_FF_CHUNK = 256
_CARRY_ROWS = SUBLANES


def _ffn_kernel(x1_ref, om_ref, wo_ref, g3_ref, wup_ref, cw_ref, cb_ref, wdn_ref, past_ref,
                y_ref, a_ref, carry_ref, *, seq_tiles, seq_len):
    i = pl.program_id(0)
    tm = x1_ref.shape[0]
    x2 = x1_ref[...] + _dot(om_ref[...], wo_ref[...])
    h = _rms_rows(x2, g3_ref[...]).astype(BF16)
    short = seq_len < tm
    row = _row_iota((tm, _FF_CHUNK))
    rin = row % seq_len if short else row
    first = (i % seq_tiles) == 0
    acc = jnp.zeros((tm, D_MODEL), F32)
    for c in range(D_FF // _FF_CHUNK):
        cols = slice(c * _FF_CHUNK, (c + 1) * _FF_CHUNK)
        a = _dot(h, wup_ref[:, cols])
        v = _dot(h, wup_ref[:, D_FF + c * _FF_CHUNK:D_FF + (c + 1) * _FF_CHUNK])
        if short:
            prev = past_ref[:, cols]
            a1 = jnp.where(rin == 0, pltpu.roll(prev, tm - 1, 0), pltpu.roll(a, 1, 0))
            a2 = jnp.where(rin < 2, prev, pltpu.roll(a, 2, 0))
            a_ref[:, cols] = a
        else:
            prev = jnp.where(first, 0.0, carry_ref[:, cols])
            prev = jnp.concatenate([prev] * (tm // _CARRY_ROWS), axis=0)
            a1 = jnp.where(rin == 0, pltpu.roll(prev, 1, 0), pltpu.roll(a, 1, 0))
            a2 = jnp.where(rin < 2, pltpu.roll(prev, 2, 0), pltpu.roll(a, 2, 0))
            carry_ref[:, cols] = a[tm - _CARRY_ROWS:, :]
            a_ref[:, cols] = a[tm - _CARRY_ROWS:, :]
        cv = cb_ref[:, cols] + cw_ref[0:1, cols] * a2 + cw_ref[1:2, cols] * a1 + cw_ref[2:3, cols] * a
        act = (cv * jax.nn.sigmoid(cv) * v).astype(BF16)
        acc = acc + _dot(act, wdn_ref[cols, :])
    y_ref[...] = x2 + acc


def _ffn(x1, o_mem, wo, g3, wup, cw, cb, wdn, past, tm, seq_len):
    n = x1.shape[0]
    short = seq_len < tm
    row = lambda w: pl.BlockSpec((tm, w), lambda i: (i, 0))
    consts = [wo, g3.reshape(1, -1), wup, cw, cb.reshape(1, -1), wdn]
    if short:
        past_spec, a_rows, a_spec = row(D_FF), n, row(D_FF)
    else:
        past = jnp.zeros((_CARRY_ROWS, D_FF), F32)
        past_spec = _const_spec(past.shape)
        a_rows = n // tm * _CARRY_ROWS
        a_spec = pl.BlockSpec((_CARRY_ROWS, D_FF), lambda i: (i, 0))
    kern = functools.partial(_ffn_kernel, seq_tiles=max(seq_len // tm, 1), seq_len=seq_len)
    return pl.pallas_call(
        kern,
        grid=(n // tm,),
        in_specs=[row(D_MODEL), row(_MEM_W)] + [_const_spec(c.shape) for c in consts] + [past_spec],
        out_specs=[row(D_MODEL), a_spec],
        out_shape=[jax.ShapeDtypeStruct((n, D_MODEL), F32), jax.ShapeDtypeStruct((a_rows, D_FF), F32)],
        scratch_shapes=[pltpu.VMEM((_CARRY_ROWS, D_FF), F32)],
        compiler_params=_cparams(("arbitrary",)),
        name="ffn",
    )(x1, o_mem, *consts, past)


def _paged_loop(pt_ref, n_chunks, pages, page_copy, body):
    s = pl.program_id(0)
    n_seq = pl.num_programs(0)

    def start(seq, chunk, slot):
        for i in range(pages):
            page_copy(pt_ref[seq, chunk * pages + i], slot, i).start()

    @pl.when(s == 0)
    def _():
        start(0, 0, 0)

    def step(j, carry):
        g = s * n_chunks + j
        slot = g % 2
        for i in range(pages):
            page_copy(0, slot, i).wait()

        @pl.when(j + 1 < n_chunks)
        def _():
            start(s, j + 1, 1 - slot)

        @pl.when((j + 1 == n_chunks) & (s + 1 < n_seq))
        def _():
            start(s + 1, 0, 1 - slot)

        body(j, slot)
        return carry

    lax.fori_loop(0, n_chunks, step, 0)


def _seq_grid_spec(n_seq, in_specs, out_specs, scratch):
    return pltpu.PrefetchScalarGridSpec(num_scalar_prefetch=1, grid=(n_seq,), in_specs=in_specs,
                                        out_specs=out_specs, scratch_shapes=scratch)


_CMP_PAGES = 32
_BLK_PER_PAGE = PAGE_SIZE // CMP_BLOCK
_TAIL_BLOCKS = SUBLANES


_PAIR_KEYS = 2 * PAGE_SIZE
_PAIR_BLOCKS = _PAIR_KEYS // CMP_BLOCK


def _compress_paged_kernel(pt_ref, cache_ref, tail_ref, perm_ref, w1_ref, b1_ref, w2_ref, kg_ref, o_ref,
                           buf_ref, xs_ref, sem_ref, *, n_chunks, pages):
    def page_copy(page, slot, i):
        return pltpu.make_async_copy(
            cache_ref.at[page, pl.ds(0, _CMP_W), :],
            buf_ref.at[slot, :, pl.ds(i * PAGE_SIZE, PAGE_SIZE)], sem_ref.at[slot])

    nb = pages * _BLK_PER_PAGE

    def body(j, slot):
        def pair(pp, carry):
            cols = buf_ref[slot, :, pl.ds(pl.multiple_of(pp * _PAIR_KEYS, _PAIR_KEYS), _PAIR_KEYS)]
            xp = _dot_t(perm_ref[...], cols.astype(BF16))
            for r in range(CMP_BLOCK):
                xs_ref[r, pl.ds(pl.multiple_of(pp * _PAIR_BLOCKS, _PAIR_BLOCKS), _PAIR_BLOCKS), :] = (
                    xp[r * _PAIR_BLOCKS:(r + 1) * _PAIR_BLOCKS])
            return carry

        lax.fori_loop(0, pages // 2, pair, 0)
        o_ref[pl.ds(pl.multiple_of(j * nb, nb), nb), :] = _compress_math(
            xs_ref, w1_ref, b1_ref, w2_ref, kg_ref, row_major=False)

    n_past = n_chunks * nb
    o_ref[n_past:n_past + _TAIL_BLOCKS, :] = _compress_math(tail_ref, w1_ref, b1_ref, w2_ref, kg_ref)
    _paged_loop(pt_ref, n_chunks, pages, page_copy, body)


def _compress_paged(page_table, cache, tail, cw):
    n_seq, n_pages = page_table.shape
    pages = min(_CMP_PAGES, n_pages)
    assert pages % 2 == 0
    n_chunks = n_pages // pages
    n_out = n_pages * _BLK_PER_PAGE + _TAIL_BLOCKS
    key = np.arange(_PAIR_KEYS)
    perm = np.zeros((_PAIR_KEYS, _PAIR_KEYS), np.float32)
    perm[(key % CMP_BLOCK) * _PAIR_BLOCKS + key // CMP_BLOCK, key] = 1.0
    consts = [jnp.asarray(perm, BF16), cw['w1'], cw['b1'], cw['w2'], cw['kg']]
    cs = lambda c: pl.BlockSpec(c.shape, lambda s, pt: (0,) * c.ndim)
    kern = functools.partial(_compress_paged_kernel, n_chunks=n_chunks, pages=pages)
    return pl.pallas_call(
        kern,
        grid_spec=_seq_grid_spec(
            n_seq,
            [pl.BlockSpec(memory_space=pl.ANY),
             pl.BlockSpec((None, _TAIL_BLOCKS, CMP_BLOCK, _CMP_W), lambda s, pt: (s, 0, 0, 0))]
            + [cs(c) for c in consts],
            pl.BlockSpec((None, n_out, _CMP_W), lambda s, pt: (s, 0, 0)),
            [pltpu.VMEM((2, _CMP_W, pages * PAGE_SIZE), F32),
             pltpu.VMEM((CMP_BLOCK, pages * _BLK_PER_PAGE, _CMP_W), F32), pltpu.SemaphoreType.DMA((2,))]),
        out_shape=jax.ShapeDtypeStruct((n_seq, n_out, _CMP_W), F32),
        compiler_params=_cparams(("arbitrary",)),
        name="compress_paged",
    )(page_table, cache, tail, *consts)


_SEL_PAGES = 16


def _nsa_sample_kernel(pt_ref, rb_ref, q_ref, gate_ref, kc_ref, cache_ref, new_ref, wst_ref, wnew_ref,
                       o_ref, buf_ref, sem_ref, ms_ref, ls_ref, as_ref, mw_ref, lw_ref, aw_ref,
                       *, n_chunks, pages, past_len, n_cmp_real, n_pick):
    tq = q_ref.shape[0]
    nrow = NSA_HEADS * tq
    kp = pages * PAGE_SIZE

    def page_copy(page, slot, i):
        return pltpu.make_async_copy(
            cache_ref.at[page, pl.ds(_WIN_W, _WIN_W), :],
            buf_ref.at[slot, :, pl.ds(i * PAGE_SIZE, PAGE_SIZE)], sem_ref.at[slot])

    q8 = _stack_heads(q_ref[...])

    def head_bias(dist):
        return jnp.concatenate([_bias_from_dist(dist, rb_ref, hd) for hd in range(NSA_HEADS)], axis=0)

    def rep_heads(x):
        return jnp.concatenate([x] * NSA_HEADS, axis=0)

    kc = kc_ref[...]
    n_cmp = kc.shape[0]
    n_sel = n_cmp // (SEL_BLOCK // CMP_BLOCK)
    q_pos_c = past_len + _row_iota((tq, n_cmp))
    blk_c = _lane_iota((tq, n_cmp))
    dist_c = q_pos_c - (blk_c * CMP_BLOCK + (CMP_BLOCK - 1))
    ok_c = rep_heads((dist_c >= 0) & (blk_c < n_cmp_real))
    s_c = jnp.where(ok_c, _dot_t(q8, kc[:, :LANES].astype(BF16)) + head_bias(dist_c), NEG_INF)
    e_c = jnp.exp(s_c - jnp.max(s_c, axis=-1, keepdims=True))
    p_c = jnp.where(ok_c, e_c / jnp.sum(e_c, axis=-1, keepdims=True), 0.0)
    o_c = _dot(p_c.astype(BF16), kc[:, LANES:].astype(BF16))

    pair = (_row_iota((n_cmp, n_sel)) // (SEL_BLOCK // CMP_BLOCK) == _lane_iota((n_cmp, n_sel)))
    pair = jnp.where(pair, 1.0, 0.0).astype(BF16)
    imp_n = []
    for g in range(NSA_GROUPS):
        acc_g = p_c[(g * NSA_HPG) * tq:(g * NSA_HPG + 1) * tq]
        for hl in range(1, NSA_HPG):
            acc_g = acc_g + p_c[(g * NSA_HPG + hl) * tq:(g * NSA_HPG + hl + 1) * tq]
        imp_n.append(acc_g)
    x = _exact_dot01(jnp.concatenate(imp_n, axis=0), pair)
    blk = _lane_iota(x.shape)
    cur = (past_len + _row_iota(x.shape) % tq) // SEL_BLOCK
    forced = (blk == 0) | (blk == cur) | (blk == cur - 1)
    x = jnp.where(forced, FORCED_SCORE, jnp.where(blk <= cur, x, -1.0))
    picked = []
    for _ in range(n_pick):
        m = jnp.max(x, axis=-1, keepdims=True)
        idx = jnp.min(jnp.where(x == m, blk, n_sel), axis=-1, keepdims=True)
        picked.append(idx)
        x = jnp.where(blk == idx, -3e38, x)

    for m_ref, l_ref, a_ref in ((ms_ref, ls_ref, as_ref), (mw_ref, lw_ref, aw_ref)):
        m_ref[...] = jnp.full(m_ref.shape, NEG_INF, F32)
        l_ref[...] = jnp.zeros(l_ref.shape, F32)
        a_ref[...] = jnp.zeros(a_ref.shape, F32)

    def picked_mask(key_blk):
        key2 = jnp.concatenate([key_blk] * NSA_GROUPS, axis=0)
        hit = jnp.zeros(key2.shape, F32)
        for idx in picked:
            hit = jnp.where(key2 == idx, 1.0, hit)
        hit = hit > 0.5
        return jnp.concatenate([hit[:tq]] * NSA_HPG + [hit[tq:]] * NSA_HPG, axis=0)

    wst = wst_ref[...]
    w_s = wst.shape[1]
    dist_w = w_s + _row_iota((tq, w_s)) - _lane_iota((tq, w_s))
    _flash_step(_dot(q8, wst[:LANES].astype(BF16)) + head_bias(dist_w),
                rep_heads((dist_w >= 0) & (dist_w < WINDOW)), wst[LANES:].astype(BF16), mw_ref, lw_ref, aw_ref,
                v_feature_major=True)
    dist_n = _row_iota((tq, tq)) - _lane_iota((tq, tq))
    bias_n = head_bias(dist_n)
    ok_n = rep_heads(dist_n >= 0)
    wnew = wnew_ref[...]
    _flash_step(_dot_t(q8, wnew[:, :LANES].astype(BF16)) + bias_n, ok_n, wnew[:, LANES:].astype(BF16),
                mw_ref, lw_ref, aw_ref)

    new = new_ref[...]
    cur_new = (past_len + _lane_iota((tq, tq))) // SEL_BLOCK
    _flash_step(_dot_t(q8, new[:, :LANES].astype(BF16)) + bias_n, ok_n & picked_mask(cur_new),
                new[:, LANES:].astype(BF16), ms_ref, ls_ref, as_ref)

    far = head_bias(jnp.full((tq, PAGE_SIZE), REL_MAX_DIST, jnp.int32))
    near = head_bias(past_len + _row_iota((tq, PAGE_SIZE)) - (past_len - PAGE_SIZE + _lane_iota((tq, PAGE_SIZE))))

    def body(j, slot):
        cols = buf_ref[slot]
        last = jnp.where(j == n_chunks - 1, near, far)
        bias = jnp.concatenate([far] * (pages - 1) + [last], axis=1)
        key_blk = (j * kp + _lane_iota((tq, kp))) // SEL_BLOCK
        _flash_step(_dot(q8, cols[:LANES].astype(BF16)) + bias, picked_mask(key_blk),
                    cols[LANES:].astype(BF16), ms_ref, ls_ref, as_ref, v_feature_major=True)

    _paged_loop(pt_ref, n_chunks, pages, page_copy, body)

    o_s = as_ref[...] / ls_ref[...]
    o_w = aw_ref[...] / lw_ref[...]
    gate = gate_ref[...]
    outs = []
    for hd in range(NSA_HEADS):
        sl = slice(hd * tq, (hd + 1) * tq)
        outs.append(gate[:, 3 * hd:3 * hd + 1] * o_c[sl] + gate[:, 3 * hd + 1:3 * hd + 2] * o_s[sl]
                    + gate[:, 3 * hd + 2:3 * hd + 3] * o_w[sl])
    o_ref[...] = _unstack_heads(jnp.concatenate(outs, axis=0), tq).astype(o_ref.dtype)


def _nsa_sample(page_table, rel_bias, qn, gate, kc, cache, rows_new, win_state, win_new, past_len):
    n_seq, n_pages = page_table.shape
    tq = qn.shape[1]
    pages = min(_SEL_PAGES, n_pages)
    n_chunks = n_pages // pages
    n_cmp = kc.shape[1]
    n_cmp_real = -(-(past_len + tq) // SEL_BLOCK) * SEL_BLOCK // CMP_BLOCK
    n_sel_real = n_cmp_real // (SEL_BLOCK // CMP_BLOCK)
    nrow = NSA_HEADS * tq
    kern = functools.partial(_nsa_sample_kernel, n_chunks=n_chunks, pages=pages, past_len=past_len,
                             n_cmp_real=n_cmp_real, n_pick=min(TOP_N, n_sel_real))
    per_seq = lambda shp, last=0: pl.BlockSpec((None,) + shp, lambda s, pt: (s, 0, last))
    stat = lambda w: pltpu.VMEM((nrow, w), F32)
    return pl.pallas_call(
        kern,
        grid_spec=_seq_grid_spec(
            n_seq,
            [pl.BlockSpec(memory_space=pltpu.SMEM), per_seq((tq, _QN_W)), per_seq((tq, LANES)),
             per_seq((n_cmp, _CMP_W)), pl.BlockSpec(memory_space=pl.ANY),
             per_seq((tq, _WIN_W), 1), per_seq((_WIN_W, win_state.shape[2])), per_seq((tq, _WIN_W))],
            per_seq((tq, _QN_W)),
            [pltpu.VMEM((2, _WIN_W, pages * PAGE_SIZE), F32), pltpu.SemaphoreType.DMA((2,)),
             stat(1), stat(1), stat(LANES), stat(1), stat(1), stat(LANES)]),
        out_shape=jax.ShapeDtypeStruct((n_seq, tq, _QN_W), BF16),
        compiler_params=_cparams(("arbitrary",)),
        name="nsa_sample",
    )(page_table, rel_bias, qn, gate, kc, cache, rows_new, win_state, win_new)


_MLA_PAGES = 16


def _mla_sample_kernel(pt_ref, q_ref, cache_ref, new_ref, tab_ref, tabn_ref, wknt_ref, wv_ref, gk_ref,
                       o_ref, buf_ref, sem_ref, m_ref, l_ref, acc_ref, *, n_chunks, pages):
    tq = o_ref.shape[0]
    nrow = MLA_HEADS * tq
    kp = pages * PAGE_SIZE

    def page_copy(page, slot, i):
        return pltpu.make_async_copy(
            cache_ref.at[page], buf_ref.at[slot, :, pl.ds(i * PAGE_SIZE, PAGE_SIZE)], sem_ref.at[slot])

    q = q_ref[...].astype(F32)
    lane = _lane_iota(q.shape)
    gk = gk_ref[...]
    qg = q * gk
    nope = jnp.where(lane < MLA_NOPE_DIM, qg, 0.0)
    nope = nope + pltpu.roll(nope, MLA_NOPE_DIM, 1)
    qn = jnp.concatenate([nope] * (MLA_HEADS * MLA_NOPE_DIM // LANES), axis=1)
    own = _lane_iota(qn.shape) // MLA_NOPE_DIM == _row_iota(qn.shape) // tq
    q_abs = _dot(jnp.where(own, qn, 0.0).astype(BF16), wknt_ref[...])
    up =pltpu.roll(q, LANES - ROPE_HALF, 1)
    dn = pltpu.roll(q, ROPE_HALF, 1)
    in_x1 = (lane >= MLA_NOPE_DIM) & (lane < MLA_NOPE_DIM + ROPE_HALF)
    in_x2 = (lane >= MLA_NOPE_DIM + ROPE_HALF) & (lane < MLA_QK_DIM)
    q_hat = jnp.where(in_x1, up, jnp.where(in_x2, -dn, 0.0)) * gk
    q_cos = pltpu.roll(jnp.where(lane >= MLA_NOPE_DIM, qg, 0.0), LANES - MLA_NOPE_DIM, 1)
    q_sin = pltpu.roll(q_hat, LANES - MLA_ROPE_DIM, 1)
    ones_rows = jnp.where((_lane_iota((SUBLANES, LANES)) >= 2 * MLA_ROPE_DIM), 1.0, 0.0)
    lhs_r = jnp.concatenate([jnp.where(lane < MLA_ROPE_DIM, q_cos, jnp.where(lane < 2 * MLA_ROPE_DIM, q_sin, 0.0)),
                             ones_rows], axis=0).astype(BF16)
    lhs_c = jnp.concatenate([wknt_ref[...], q_abs.astype(BF16)], axis=0)
    n_kn = MLA_HEADS * MLA_NOPE_DIM

    m_ref[...] = jnp.full(m_ref.shape, NEG_INF, F32)
    l_ref[...] = jnp.zeros(l_ref.shape, F32)
    acc_ref[...] = jnp.zeros(acc_ref.shape, F32)

    def attend(cols, tab, allowed):
        nk = cols.shape[1]
        c = cols[:MLA_KV_RANK].astype(BF16)
        kr = cols[MLA_KV_RANK:]
        sq = kr * kr
        hi = sq.astype(BF16).astype(F32)
        rhs = (jnp.concatenate([kr, kr, hi, sq - hi], axis=0) * tab).astype(BF16)
        sr = _dot(lhs_r, rhs)
        both = _dot(lhs_c, c)
        inv = []
        for hd in range(MLA_HEADS):
            blk = both[hd * MLA_NOPE_DIM:(hd + 1) * MLA_NOPE_DIM]
            ms = (jnp.sum(blk * blk, axis=0, keepdims=True) + sr[nrow:nrow + 1]) * (1.0 / MLA_QK_DIM)
            inv.append(jnp.broadcast_to(lax.rsqrt(ms + NORM_EPS), (tq, nk)))
        s = (both[n_kn:] + sr[:nrow]) * jnp.concatenate(inv, axis=0)
        if allowed is not None:
            s = jnp.where(allowed, s, NEG_INF)
        m_old = m_ref[...]
        m_new = jnp.maximum(m_old, jnp.max(s, axis=-1, keepdims=True))
        alpha = jnp.exp(m_old - m_new)
        p = jnp.exp(s - m_new)
        l_ref[...] = alpha * l_ref[...] + jnp.sum(p, axis=-1, keepdims=True)
        acc_ref[...] = alpha * acc_ref[...] + _dot_t(p.astype(BF16), c)
        m_ref[...] = m_new

    causal = _row_iota((tq, tq)) >= _lane_iota((tq, tq))
    attend(new_ref[...], tabn_ref[...], jnp.concatenate([causal] * MLA_HEADS, axis=0))

    def body(j, slot):
        attend(buf_ref[slot], tab_ref[:, pl.ds(pl.multiple_of(j * kp, kp), kp)], None)

    _paged_loop(pt_ref, n_chunks, pages, page_copy, body)

    o = acc_ref[...] / l_ref[...]
    outs = [_dot(o[hd * tq:(hd + 1) * tq].astype(BF16), wv_ref[hd]) for hd in range(MLA_HEADS)]
    o_ref[...] = jnp.concatenate(outs, axis=1).astype(o_ref.dtype)


def _mla_key_tables(pos):
    inv = jnp.power(ROPE_BASE, -jnp.arange(ROPE_HALF, dtype=F32) / ROPE_HALF)
    ang = pos.astype(F32)[:, None] * inv[None, :]
    cos, sin = jnp.cos(ang).T, jnp.sin(ang).T
    return jnp.concatenate([cos, cos, sin, sin, jnp.ones((LANES - 2 * MLA_ROPE_DIM, pos.shape[0]), F32)], axis=0)


def _mla_sample(page_table, q_rows, cache, rows_new, tab_past, tab_new, wknt, wv, gk):
    n_seq, n_pages = page_table.shape
    tq = rows_new.shape[2]
    nrow = MLA_HEADS * tq
    pages = min(_MLA_PAGES, n_pages)
    n_chunks = n_pages // pages
    kern = functools.partial(_mla_sample_kernel, n_chunks=n_chunks, pages=pages)
    per_seq = lambda shp: pl.BlockSpec((None,) + shp, lambda s, pt: (s, 0, 0))
    cs = lambda c: pl.BlockSpec(c.shape, lambda s, pt: (0,) * c.ndim)
    ow = MLA_HEADS * MLA_V_DIM
    return pl.pallas_call(
        kern,
        grid_spec=_seq_grid_spec(
            n_seq,
            [per_seq((nrow, LANES)), pl.BlockSpec(memory_space=pl.ANY), per_seq((_MLA_ROW_W, tq)),
             cs(tab_past), cs(tab_new), cs(wknt), cs(wv), cs(gk)],
            per_seq((tq, ow)),
            [pltpu.VMEM((2, _MLA_ROW_W, pages * PAGE_SIZE), F32), pltpu.SemaphoreType.DMA((2,)),
             pltpu.VMEM((nrow, 1), F32), pltpu.VMEM((nrow, 1), F32), pltpu.VMEM((nrow, MLA_KV_RANK), F32)]),
        out_shape=jax.ShapeDtypeStruct((n_seq, tq, ow), BF16),
        compiler_params=_cparams(("arbitrary",)),
        name="mla_sample",
    )(page_table, q_rows, cache, rows_new, tab_past, tab_new, wknt, wv, gk)


_MEM_SEQS_PER_STEP = 8


def _token_tile(n):
    return min(_TM, n)


def _after_mixers(x, o_n, o_m, mem_kv, mem_tokens, mem_seqs, lw, past, seq_len):
    tm = _token_tile(x.shape[0])
    x1, qmem = _merge(x, o_n, o_m, lw['g1'], lw['wmg'], lw['wno'], lw['wmo'], lw['wout'],
                      lw['g2'], lw['mem_wq'], lw['mem_qg'], tm)
    om = _mem_attn(qmem, mem_kv, mem_tokens, mem_seqs)
    return _ffn(x1, om, lw['mem_wo'], lw['g3'], lw['wup'], lw['conv_w'], lw['conv_b'], lw['wdn'],
                past, tm, seq_len)


def kernel(x_prompt, x_sample, cache_nsa, cache_mla, state_win, cache_mem, state_conv, page_table,
           mem_prompt, rel_bias, norm1_g, w_in, nsa_q_norm, nsa_k_norm, nsa_cmp_w1, nsa_cmp_b1,
           nsa_cmp_w2, nsa_w_o, mla_q_norm, mla_kv_norm, mla_w_uq, mla_w_ukv, mla_qk_q_norm,
           mla_qk_k_norm, mla_w_o, w_out, norm2_g, mem_norm_g, mem_w_q, mem_w_kv, mem_q_norm,
           mem_k_norm, mem_w_o, norm3_g, ffn_w_up, ffn_conv_w, ffn_conv_b, ffn_w_down):
    depth = w_in.shape[0]
    bsz, t, _ = x_prompt.shape
    n_seq, tq, _ = x_sample.shape
    n_pool = cache_nsa.shape[1]
    n_pages = page_table.shape[1]
    past_len = n_pages * PAGE_SIZE
    w_s = state_win.shape[2]
    mem_len = mem_prompt.shape[1]
    assert t % _MLA_TK == 0 and _MLA_TK % _TM == 0 and past_len % SEL_BLOCK == 0 and tq <= SEL_BLOCK and tq % SUBLANES == 0
    layer = lambda a, l: a.reshape(a.shape[1:]) if depth == 1 else a[l]

    xp = x_prompt.reshape(bsz * t, D_MODEL)
    xs = x_sample.reshape(n_seq * tq, D_MODEL)
    tm_s = _token_tile(n_seq * tq)
    tabs_p = _rope_tables(jnp.arange(t, dtype=jnp.int32))
    pos_s = past_len + jnp.arange(tq, dtype=jnp.int32)
    tabs_s = tuple(jnp.tile(tb, (tm_s // tq, 1)) for tb in _rope_tables(pos_s))
    key_tab_past = _mla_key_tables(jnp.arange(past_len, dtype=jnp.int32))
    key_tab_new = _mla_key_tables(pos_s)
    cache_nsa_fm = jnp.transpose(cache_nsa, (0, 1, 3, 4, 5, 2)).reshape(depth * n_pool, _ROWS_W, PAGE_SIZE)
    cache_mla_fm = jnp.transpose(cache_mla, (0, 1, 3, 2)).reshape(depth * n_pool, _MLA_ROW_W, PAGE_SIZE)
    mem_flat = mem_prompt.reshape(bsz * mem_len, D_MODEL)

    outs = [[] for _ in range(9)]
    for l in range(depth):
        pw = _proj_weights(w_in[l], nsa_q_norm[l], nsa_k_norm[l], mla_q_norm[l], mla_kv_norm[l],
                           mla_w_uq[l], mla_qk_q_norm[l], mla_w_ukv[l], mla_qk_k_norm[l])
        cw = _compress_weights(nsa_cmp_w1[l], nsa_cmp_b1[l], nsa_cmp_w2[l], nsa_k_norm[l, 0])
        wno = nsa_w_o[l].reshape(NSA_GROUPS, NSA_HPG, NSA_HEAD_DIM, D_MODEL)
        wno = jnp.transpose(wno, (1, 0, 2, 3)).reshape(_QN_W, D_MODEL)
        lw = dict(g1=norm1_g[l], wmg=w_in[l][:, -2 * D_MODEL:].astype(BF16), wno=wno.astype(BF16),
                  wmo=mla_w_o[l].astype(BF16), wout=w_out[l].astype(BF16), g2=norm2_g[l],
                  mem_wq=mem_w_q[l].astype(BF16), mem_qg=mem_q_norm[l], mem_wo=mem_w_o[l].astype(BF16),
                  g3=norm3_g[l], wup=ffn_w_up[l].astype(BF16), conv_w=ffn_conv_w[l], conv_b=ffn_conv_b[l],
                  wdn=ffn_w_down[l].astype(BF16))

        pr = _project(xp, tabs_p, t // _TM, norm1_g[l], pw, _TM)
        nb_p = bsz * t // CMP_BLOCK
        kc_p = _compress_rows(pr['rows'].reshape(nb_p, CMP_BLOCK, _ROWS_W), cw, min(LANES, nb_p))
        per_b = lambda a: a.reshape(bsz, t, a.shape[-1])
        o_n = _nsa_prompt(per_b(pr['qn']), per_b(pr['gate']), kc_p.reshape(bsz, t // CMP_BLOCK, _CMP_W),
                          per_b(pr['rows']), per_b(pr['win']), rel_bias)
        o_m = _mla_prompt(per_b(pr['qm']), per_b(pr['kmla']), per_b(pr['cb']), pw['wv'])
        mkv = _mem_kv(mem_flat, mem_norm_g[l], mem_w_kv[l].astype(BF16), mem_k_norm[l])
        xp, a_tail = _after_mixers(xp, o_n.reshape(bsz * t, -1), o_m.reshape(bsz * t, -1),
                                   mkv.reshape(bsz, mem_len, 2 * _MEM_W), _TM, 1, lw, None, t)
        outs[0].append(pr['rows'].reshape(bsz, t, 4, NSA_GROUPS, NSA_HEAD_DIM))
        outs[1].append(pr['mrow'].reshape(bsz, t, _MLA_ROW_W))
        outs[2].append(pr['win'].reshape(bsz, t, 2, NSA_GROUPS, NSA_HEAD_DIM)[:, -min(WINDOW, t):])
        outs[3].append(mkv.reshape(bsz, mem_len, 2, MEM_HEADS, MEM_HEAD_DIM))
        outs[4].append(a_tail.reshape(bsz, t // _TM, _CARRY_ROWS, D_FF)[:, -1, _CARRY_ROWS - (CONV_W - 1):])

        sr = _project(xs, tabs_s, 1, norm1_g[l], pw, tm_s)
        rows_s = sr['rows'].reshape(n_seq, tq, _ROWS_W)
        win_new = sr['win'].reshape(n_seq, tq, _WIN_W)
        pt_l = page_table + l * n_pool
        tail = jnp.pad(rows_s, ((0, 0), (0, _TAIL_BLOCKS * CMP_BLOCK - tq), (0, 0)))
        kc_s = _compress_paged(pt_l, cache_nsa_fm, tail.reshape(n_seq, _TAIL_BLOCKS, CMP_BLOCK, _ROWS_W), cw)
        win_old = jnp.transpose(layer(state_win, l), (0, 2, 3, 4, 1)).reshape(n_seq, _WIN_W, w_s)
        o_n = _nsa_sample(pt_l, rel_bias, sr['qn'].reshape(n_seq, tq, _QN_W), sr['gate'].reshape(n_seq, tq, LANES),
                          kc_s, cache_nsa_fm, rows_s, win_old, win_new, past_len)
        q_rows = jnp.transpose(sr['qm'].reshape(n_seq, tq, MLA_HEADS, MLA_HEAD_PAD), (0, 2, 1, 3))
        mrow_s = sr['mrow'].reshape(n_seq, tq, _MLA_ROW_W)
        o_m = _mla_sample(pt_l, q_rows.reshape(n_seq, MLA_HEADS * tq, MLA_HEAD_PAD), cache_mla_fm,
                          jnp.transpose(mrow_s, (0, 2, 1)), key_tab_past, key_tab_new,
                          pw['wkn_flat'].T, pw['wv'], pw['qkk'])
        past = jnp.pad(layer(state_conv, l), ((0, 0), (0, tq - (CONV_W - 1)), (0, 0)))
        xs, a_s = _after_mixers(xs, o_n.reshape(n_seq * tq, -1), o_m.reshape(n_seq * tq, -1),
                                layer(cache_mem, l).reshape(n_seq, mem_len, 2 * _MEM_W), tq,
                                min(_MEM_SEQS_PER_STEP, n_seq), lw, past.reshape(n_seq * tq, D_FF), tq)
        outs[5].append(rows_s.reshape(n_seq, tq, 4, NSA_GROUPS, NSA_HEAD_DIM))
        outs[6].append(mrow_s)
        win_all = jnp.concatenate([win_old, jnp.transpose(win_new, (0, 2, 1))], axis=2)[:, :, -w_s:]
        win_all = win_all.reshape(n_seq, 2, NSA_GROUPS, NSA_HEAD_DIM, w_s)
        outs[7].append(jnp.transpose(win_all, (0, 4, 1, 2, 3)))
        outs[8].append(a_s.reshape(n_seq, tq, D_FF)[:, tq - (CONV_W - 1):])

    return (xp.reshape(bsz, t, D_MODEL), xs.reshape(n_seq, tq, D_MODEL)) + tuple(jnp.stack(o) for o in outs)
```

```python
import functools
import math

import numpy as np
import jax
import jax.numpy as jnp
from jax import lax
from jax.experimental import pallas as pl
from jax.experimental.pallas import tpu as pltpu

D_MODEL = 1024
PAGE_SIZE = 128
NSA_HEADS = 8
NSA_GROUPS = 2
NSA_HPG = NSA_HEADS // NSA_GROUPS
NSA_HEAD_DIM = 64
CMP_BLOCK = 32
SEL_BLOCK = 64
TOP_N = 16
WINDOW = 512
MLA_HEADS = 8
MLA_Q_RANK = 384
MLA_KV_RANK = 256
MLA_NOPE_DIM = 64
MLA_ROPE_DIM = 32
MLA_QK_DIM = MLA_NOPE_DIM + MLA_ROPE_DIM
MLA_V_DIM = 64
ROPE_BASE = 10000.0
MEM_HEADS = 4
MEM_HEAD_DIM = 128
D_FF = 2816
CONV_W = 3
REL_BUCKETS = 32
REL_MAX_DIST = 128
NORM_EPS = 1e-6
NEG_INF = -1e30
FORCED_SCORE = 1e9

LANES = 128
SUBLANES = 8
VMEM_LIMIT = 56 * 1024 * 1024

HALF = LANES // 2
MLA_HEAD_PAD = LANES
ROPE_HALF = MLA_ROPE_DIM // 2

F32 = jnp.float32
BF16 = jnp.bfloat16


def _cparams(sem):
    return pltpu.CompilerParams(dimension_semantics=sem, vmem_limit_bytes=VMEM_LIMIT)


def _const_spec(shape):
    nd = len(shape)
    return pl.BlockSpec(shape, lambda *_: (0,) * nd)


def _dot(a, b):
    return jnp.dot(a, b, preferred_element_type=F32)


def _dot_t(a, b):
    return lax.dot_general(a, b, (((1,), (1,)), ((), ())), preferred_element_type=F32)


def _lane_iota(shape):
    return lax.broadcasted_iota(jnp.int32, shape, len(shape) - 1)


def _row_iota(shape):
    return lax.broadcasted_iota(jnp.int32, shape, len(shape) - 2)


def _rms_rows(x, gain):
    ms = jnp.mean(x * x, axis=-1, keepdims=True)
    return x * lax.rsqrt(ms + NORM_EPS) * gain


def _rms_half_lanes(x, gain):
    low = _lane_iota(x.shape) < HALF
    x2 = x * x
    s_lo = jnp.sum(jnp.where(low, x2, 0.0), axis=-1, keepdims=True)
    s_hi = jnp.sum(jnp.where(low, 0.0, x2), axis=-1, keepdims=True)
    ms = jnp.where(low, s_lo, s_hi) * (1.0 / HALF)
    return x * lax.rsqrt(ms + NORM_EPS) * gain


def _rope_block(x, c, s1, s2):
    up = pltpu.roll(x, LANES - ROPE_HALF, 1)
    dn = pltpu.roll(x, ROPE_HALF, 1)
    return x * c + up * s1 + dn * s2


def _rel_bucket_upper():
    exact = REL_BUCKETS // 2
    ub = []
    for b in range(REL_BUCKETS - 1):
        if b < exact:
            ub.append(b)
        else:
            d = b
            while True:
                nxt = d + 1
                lg = exact + int(math.log(nxt / exact) / math.log(REL_MAX_DIST / exact) * (REL_BUCKETS - exact))
                if min(lg, REL_BUCKETS - 1) > b:
                    break
                d = nxt
            ub.append(d)
    return ub


_BUCKET_UB = _rel_bucket_upper()


def _bias_from_dist(dist, rb_ref, head):
    out = jnp.full(dist.shape, rb_ref[REL_BUCKETS - 1, head], F32)
    for b in range(REL_BUCKETS - 2, -1, -1):
        out = jnp.where(dist <= _BUCKET_UB[b], rb_ref[b, head], out)
    return out


_QN_W = NSA_HEADS * NSA_HEAD_DIM
_KV_W = 6 * NSA_GROUPS * NSA_HEAD_DIM
_PROJ_W = _QN_W + _KV_W + MLA_Q_RANK + MLA_KV_RANK + 2 * LANES
_O_KV = _QN_W
_O_CQ = _O_KV + _KV_W
_O_CKV = _O_CQ + MLA_Q_RANK
_O_KR = _O_CKV + MLA_KV_RANK
_O_G = _O_KR + LANES
_MLA_PAD_W = MLA_HEADS * MLA_HEAD_PAD
_ROWS_W = 4 * NSA_GROUPS * NSA_HEAD_DIM
_WIN_W = 2 * NSA_GROUPS * NSA_HEAD_DIM
_MLA_ROW_W = MLA_KV_RANK + MLA_ROPE_DIM


def _proj_kernel(x_ref, tc_ref, ts1_ref, ts2_ref, g1_ref, w_ref, qg_ref, ksg_ref, kwg_ref,
                 cqg_ref, ckvg_ref, wuq_ref, qkq_ref, wkn_ref, qkk_ref,
                 qn_ref, rows_ref, win_ref, gate_ref, qm_ref, mrow_ref, kmla_ref, cb_ref):
    x = x_ref[...]
    h = _rms_rows(x, g1_ref[...]).astype(BF16)
    y = _dot(h, w_ref[...])

    nsa_scale = NSA_HEAD_DIM ** -0.5
    for v in range(_QN_W // LANES):
        blk = y[:, v * LANES:(v + 1) * LANES]
        qn_ref[:, v * LANES:(v + 1) * LANES] = (_rms_half_lanes(blk, qg_ref[...]) * nsa_scale).astype(BF16)

    kv = lambda slot: y[:, _O_KV + slot * LANES:_O_KV + (slot + 1) * LANES]
    rows_ref[:, 0:LANES] = kv(0)
    rows_ref[:, LANES:2 * LANES] = kv(1)
    rows_ref[:, 2 * LANES:3 * LANES] = _rms_half_lanes(kv(2), ksg_ref[...])
    rows_ref[:, 3 * LANES:4 * LANES] = kv(3)
    win_ref[:, 0:LANES] = _rms_half_lanes(kv(4), kwg_ref[...])
    win_ref[:, LANES:2 * LANES] = kv(5)
    gate_ref[...] = jax.nn.sigmoid(y[:, _O_G:_O_G + LANES])

    tc, ts1, ts2 = tc_ref[...], ts1_ref[...], ts2_ref[...]
    mla_scale = MLA_QK_DIM ** -0.5
    inv_qk = 1.0 / MLA_QK_DIM

    cq = _rms_rows(y[:, _O_CQ:_O_CQ + MLA_Q_RANK], cqg_ref[...]).astype(BF16)
    qraw = _dot(cq, wuq_ref[...])
    for hd in range(MLA_HEADS):
        blk = qraw[:, hd * LANES:(hd + 1) * LANES]
        ms = jnp.sum(blk * blk, axis=-1, keepdims=True) * inv_qk
        blk = blk * lax.rsqrt(ms + NORM_EPS) * qkq_ref[...]
        qm_ref[:, hd * LANES:(hd + 1) * LANES] = (_rope_block(blk, tc, ts1, ts2) * mla_scale).astype(BF16)

    c = _rms_rows(y[:, _O_CKV:_O_CKV + MLA_KV_RANK], ckvg_ref[...])
    krv = y[:, _O_KR:_O_KR + LANES]
    mrow_ref[:, 0:MLA_KV_RANK] = c
    mrow_ref[:, MLA_KV_RANK:_MLA_ROW_W] = krv[:, 0:MLA_ROPE_DIM]
    cbf = c.astype(BF16)
    cb_ref[...] = cbf

    kr_at = pltpu.roll(jnp.where(_lane_iota(krv.shape) < MLA_ROPE_DIM, krv, 0.0), MLA_NOPE_DIM, 1)
    ss_r = jnp.sum(kr_at * kr_at, axis=-1, keepdims=True)
    kr_rot = _rope_block(kr_at * qkk_ref[...], tc, ts1, ts2)
    kn = _dot(cbf, wkn_ref[...])
    for hd in range(MLA_HEADS):
        blk = kn[:, hd * LANES:(hd + 1) * LANES]
        ms = (jnp.sum(blk * blk, axis=-1, keepdims=True) + ss_r) * inv_qk
        kmla_ref[:, hd * LANES:(hd + 1) * LANES] = (
            (blk * qkk_ref[...] + kr_rot) * lax.rsqrt(ms + NORM_EPS)).astype(BF16)


def _rope_tables(pos):
    inv = jnp.power(ROPE_BASE, -jnp.arange(ROPE_HALF, dtype=F32) / ROPE_HALF)
    ang = pos.astype(F32)[:, None] * inv[None, :]
    cos, sin = jnp.cos(ang), jnp.sin(ang)
    n = pos.shape[0]
    z = lambda w: jnp.zeros((n, w), F32)
    pad = MLA_HEAD_PAD - MLA_QK_DIM
    tc = jnp.concatenate([jnp.ones((n, MLA_NOPE_DIM), F32), cos, cos, z(pad)], axis=1)
    ts1 = jnp.concatenate([z(MLA_NOPE_DIM), -sin, z(ROPE_HALF), z(pad)], axis=1)
    ts2 = jnp.concatenate([z(MLA_NOPE_DIM), z(ROPE_HALF), sin, z(pad)], axis=1)
    return tc, ts1, ts2


def _pad_heads(w, n_heads, width, used):
    k = w.shape[0]
    w = w.reshape(k, n_heads, width)[:, :, :used]
    return jnp.pad(w, ((0, 0), (0, 0), (0, MLA_HEAD_PAD - used))).reshape(k, n_heads * MLA_HEAD_PAD)


def _mla_gain_pad(g):
    return jnp.pad(g, (0, MLA_HEAD_PAD - MLA_QK_DIM)).reshape(1, MLA_HEAD_PAD)


def _proj_weights(w_in, nsa_q_norm, nsa_k_norm, mla_q_norm, mla_kv_norm, mla_w_uq, mla_qk_q_norm,
                  mla_w_ukv, mla_qk_k_norm):
    o = np.cumsum((0, _QN_W, _KV_W, 3 * NSA_HEADS, MLA_Q_RANK, MLA_KV_RANK, MLA_ROPE_DIM))
    qn = w_in[:, o[0]:o[1]].reshape(D_MODEL, NSA_GROUPS, NSA_HPG, NSA_HEAD_DIM)
    qn = jnp.transpose(qn, (0, 2, 1, 3)).reshape(D_MODEL, _QN_W)
    padc = lambda w: jnp.pad(w, ((0, 0), (0, LANES - w.shape[1])))
    w = jnp.concatenate([qn, w_in[:, o[1]:o[2]], w_in[:, o[3]:o[4]], w_in[:, o[4]:o[5]],
                         padc(w_in[:, o[5]:o[6]]), padc(w_in[:, o[2]:o[3]])], axis=1).astype(BF16)
    tile2 = lambda g: jnp.tile(g, NSA_GROUPS).reshape(1, LANES)
    wkn = mla_w_ukv.reshape(MLA_KV_RANK, MLA_HEADS, MLA_NOPE_DIM + MLA_V_DIM)
    wv = jnp.transpose(wkn[:, :, MLA_NOPE_DIM:], (1, 0, 2)).astype(BF16)
    wkn_flat = wkn[:, :, :MLA_NOPE_DIM].reshape(MLA_KV_RANK, MLA_HEADS * MLA_NOPE_DIM)
    return dict(
        w=w, qg=tile2(nsa_q_norm), ksg=tile2(nsa_k_norm[1]), kwg=tile2(nsa_k_norm[2]),
        cqg=mla_q_norm.reshape(1, -1), ckvg=mla_kv_norm.reshape(1, -1),
        wuq=_pad_heads(mla_w_uq, MLA_HEADS, MLA_QK_DIM, MLA_QK_DIM).astype(BF16),
        qkq=_mla_gain_pad(mla_qk_q_norm),
        wkn=_pad_heads(wkn_flat, MLA_HEADS, MLA_NOPE_DIM, MLA_NOPE_DIM).astype(BF16),
        qkk=_mla_gain_pad(mla_qk_k_norm), wv=wv, wkn_flat=wkn_flat.astype(BF16))


def _project(x, pos_tables, n_pos_tiles, g1, pw, tm):
    n = x.shape[0]
    grid = (n // tm,)
    row = lambda w: pl.BlockSpec((tm, w), lambda i: (i, 0))
    tab = pl.BlockSpec((tm, LANES), lambda i: (i % n_pos_tiles, 0))
    consts = [g1.reshape(1, -1), pw['w'], pw['qg'], pw['ksg'], pw['kwg'], pw['cqg'], pw['ckvg'],
              pw['wuq'], pw['qkq'], pw['wkn'], pw['qkk']]
    outs = pl.pallas_call(
        _proj_kernel,
        grid=grid,
        in_specs=[row(D_MODEL), tab, tab, tab] + [_const_spec(c.shape) for c in consts],
        out_specs=[row(_QN_W), row(_ROWS_W), row(_WIN_W), row(LANES), row(_MLA_PAD_W),
                   row(_MLA_ROW_W), row(_MLA_PAD_W), row(MLA_KV_RANK)],
        out_shape=[jax.ShapeDtypeStruct((n, _QN_W), BF16), jax.ShapeDtypeStruct((n, _ROWS_W), F32),
                   jax.ShapeDtypeStruct((n, _WIN_W), F32), jax.ShapeDtypeStruct((n, LANES), F32),
                   jax.ShapeDtypeStruct((n, _MLA_PAD_W), BF16), jax.ShapeDtypeStruct((n, _MLA_ROW_W), F32),
                   jax.ShapeDtypeStruct((n, _MLA_PAD_W), BF16), jax.ShapeDtypeStruct((n, MLA_KV_RANK), BF16)],
        compiler_params=_cparams(("arbitrary",)),
        name="proj",
    )(x, *pos_tables, *consts)
    return dict(zip(('qn', 'rows', 'win', 'gate', 'qm', 'mrow', 'kmla', 'cb'), outs))


_CMP_W = 2 * NSA_GROUPS * NSA_HEAD_DIM


def _compress_math(x_ref, w1_ref, b1_ref, w2_ref, kg_ref, row_major=True):
    if row_major:
        rows_at = lambda r: x_ref[:, r, :]
    else:
        nb = x_ref.shape[0] * x_ref.shape[2]
        rows_at = lambda r: x_ref[:, r, :, :].reshape(nb, x_ref.shape[3])
    acc = _dot(rows_at(0).astype(BF16), w1_ref[0])
    for r in range(1, CMP_BLOCK):
        acc = acc + _dot(rows_at(r).astype(BF16), w1_ref[r])
    y = acc + b1_ref[...]
    y = y * jax.nn.sigmoid(y)
    z = _dot(y.astype(BF16), w2_ref[...])
    return jnp.concatenate([_rms_half_lanes(z[:, :LANES], kg_ref[...]), z[:, LANES:]], axis=1)


def _compress_kernel(x_ref, w1_ref, b1_ref, w2_ref, kg_ref, o_ref):
    o_ref[...] = _compress_math(x_ref, w1_ref, b1_ref, w2_ref, kg_ref)


def _compress_weights(w1, b1, w2, kc_norm):
    eye = jnp.eye(NSA_GROUPS, dtype=F32)
    w1r = w1.reshape(2, CMP_BLOCK, NSA_HEAD_DIM, NSA_HEAD_DIM)
    w1e = jnp.einsum('crde,cx,gy->rcgdxye', w1r, jnp.eye(2, dtype=F32), eye)
    w1e = w1e.reshape(CMP_BLOCK, _CMP_W, _CMP_W).astype(BF16)
    w2e = jnp.einsum('cde,cx,gy->cgdxye', w2, jnp.eye(2, dtype=F32), eye).reshape(_CMP_W, _CMP_W).astype(BF16)
    b1e = jnp.broadcast_to(b1[:, None, :], (2, NSA_GROUPS, NSA_HEAD_DIM)).reshape(1, _CMP_W)
    return dict(w1=w1e, b1=b1e, w2=w2e, kg=jnp.tile(kc_norm, NSA_GROUPS).reshape(1, LANES))


def _compress_rows(rows, cw, nbt):
    nb = rows.shape[0]
    consts = [cw['w1'], cw['b1'], cw['w2'], cw['kg']]
    return pl.pallas_call(
        _compress_kernel,
        grid=(nb // nbt,),
        in_specs=[pl.BlockSpec((nbt, CMP_BLOCK, _CMP_W), lambda i: (i, 0, 0))]
        + [_const_spec(c.shape) for c in consts],
        out_specs=pl.BlockSpec((nbt, _CMP_W), lambda i: (i, 0)),
        out_shape=jax.ShapeDtypeStruct((nb, _CMP_W), F32),
        compiler_params=_cparams(("arbitrary",)),
        name="compress",
    )(rows, *consts)


_TQ = 128
_NROW = NSA_HEADS * _TQ


def _stack_heads(q):
    low = _lane_iota((q.shape[0], LANES)) < HALF
    zero = jnp.zeros((), q.dtype)
    blocks = [q[:, v * LANES:(v + 1) * LANES] for v in range(NSA_HPG)]
    return jnp.concatenate([jnp.where(low, b, zero) for b in blocks]
                           + [jnp.where(low, zero, b) for b in blocks], axis=0)


def _unstack_heads(o, tq):
    low = _lane_iota((tq, LANES)) < HALF
    return jnp.concatenate(
        [jnp.where(low, o[v * tq:(v + 1) * tq], o[(NSA_HPG + v) * tq:(NSA_HPG + v + 1) * tq])
         for v in range(NSA_HPG)], axis=1)


def _exact_dot01(x, m01):
    a = x.astype(BF16)
    r1 = x - a.astype(F32)
    b = r1.astype(BF16)
    c = (r1 - b.astype(F32)).astype(BF16)
    return _dot(a, m01) + _dot(b, m01) + _dot(c, m01)


def _flash_step(s, allowed, v, m_ref, l_ref, acc_ref, v_feature_major=False):
    s = jnp.where(allowed, s, NEG_INF)
    m_old = m_ref[...]
    m_new = jnp.maximum(m_old, jnp.max(s, axis=-1, keepdims=True))
    alpha = jnp.exp(m_old - m_new)
    p = jnp.exp(s - m_new)
    l_ref[...] = alpha * l_ref[...] + jnp.sum(p, axis=-1, keepdims=True)
    pb = p.astype(BF16)
    acc_ref[...] = alpha * acc_ref[...] + (_dot_t(pb, v) if v_feature_major else _dot(pb, v))
    m_ref[...] = m_new


def _exact_dot01_rhs(m01, x):
    a = x.astype(BF16)
    r1 = x - a.astype(F32)
    b = r1.astype(BF16)
    c = (r1 - b.astype(F32)).astype(BF16)
    return _dot(m01, a) + _dot(m01, b) + _dot(m01, c)


def _flash_step_cols(s, allowed, vt, m_ref, l_ref, acc_ref):
    s = jnp.where(allowed, s, NEG_INF)
    m_old = m_ref[...]
    m_new = jnp.maximum(m_old, jnp.max(s, axis=0, keepdims=True))
    alpha = jnp.exp(m_old - m_new)
    p = jnp.exp(s - m_new)
    l_ref[...] = alpha * l_ref[...] + jnp.sum(p, axis=0, keepdims=True)
    acc_ref[...] = alpha * acc_ref[...] + _dot(vt, p.astype(BF16))
    m_ref[...] = m_new


def _nsa_prompt_kernel(rb_ref, q_ref, gate_ref, kc_ref, sel_ref, win_ref, o_ref,
                       bias_ref, vst_ref, vwt_ref, ms_ref, ls_ref, as_ref, mw_ref, lw_ref, aw_ref,
                       *, n_sel, n_pick):
    b = pl.program_id(0)
    qi = pl.program_id(1)
    tq = _TQ
    n_tiles = sel_ref.shape[0] // tq
    k_loc = _row_iota((tq, tq))
    t_loc = _lane_iota((tq, tq))
    heads_on_lanes = lambda x: jnp.concatenate([x] * NSA_HEADS, axis=1)

    @pl.when((b == 0) & (qi == 0))
    def _():
        for kind in range(3):
            dist = t_loc - k_loc + kind * tq
            for hd in range(NSA_HEADS):
                bias_ref[kind, :, hd * tq:(hd + 1) * tq] = _bias_from_dist(dist, rb_ref, hd)

    @pl.when(qi == 0)
    def _():
        for i in range(n_tiles):
            rows = slice(i * tq, (i + 1) * tq)
            vst_ref[:, rows] = sel_ref[rows, LANES:].T.astype(BF16)
            vwt_ref[:, rows] = win_ref[rows, LANES:].T.astype(BF16)

    qt = q_ref[...].astype(F32).T
    low = _row_iota((LANES, tq)) < HALF
    pairs = [qt[v * LANES:(v + 1) * LANES] for v in range(NSA_HPG)]
    q8t = jnp.concatenate([jnp.where(low, p, 0.0) for p in pairs]
                          + [jnp.where(low, 0.0, p) for p in pairs], axis=1).astype(BF16)

    kc = kc_ref[...]
    n_cmp = kc.shape[0]
    dist_c = (qi * tq + _lane_iota((n_cmp, tq))) - (_row_iota((n_cmp, tq)) * CMP_BLOCK + (CMP_BLOCK - 1))
    valid_c = heads_on_lanes(dist_c >= 0)
    bias_c = jnp.concatenate([_bias_from_dist(dist_c, rb_ref, hd) for hd in range(NSA_HEADS)], axis=1)
    s_c = jnp.where(valid_c, _dot(kc[:, :LANES].astype(BF16), q8t) + bias_c, NEG_INF)
    e_c = jnp.exp(s_c - jnp.max(s_c, axis=0, keepdims=True))
    p_c = jnp.where(valid_c, e_c / jnp.sum(e_c, axis=0, keepdims=True), 0.0)
    o_c = _dot(kc[:, LANES:].T.astype(BF16), p_c.astype(BF16))

    imp_n = []
    for g in range(NSA_GROUPS):
        acc_g = p_c[:, (g * NSA_HPG) * tq:(g * NSA_HPG + 1) * tq]
        for hl in range(1, NSA_HPG):
            acc_g = acc_g + p_c[:, (g * NSA_HPG + hl) * tq:(g * NSA_HPG + hl + 1) * tq]
        imp_n.append(acc_g)
    pair = (_lane_iota((n_sel, n_cmp)) // (SEL_BLOCK // CMP_BLOCK) == _row_iota((n_sel, n_cmp)))
    imp = _exact_dot01_rhs(jnp.where(pair, 1.0, 0.0).astype(BF16), jnp.concatenate(imp_n, axis=1))
    blk = _row_iota(imp.shape)
    cur = (qi * tq + _lane_iota(imp.shape) % tq) // SEL_BLOCK
    forced = (blk == 0) | (blk == cur) | (blk == cur - 1)
    x = jnp.where(forced, FORCED_SCORE, jnp.where(blk <= cur, imp, -1.0))
    sel = jnp.zeros(imp.shape, F32)
    for _ in range(n_pick):
        m = jnp.max(x, axis=0, keepdims=True)
        idx = jnp.min(jnp.where(x == m, blk, n_sel), axis=0, keepdims=True)
        hit = blk == idx
        sel = jnp.where(hit, 1.0, sel)
        x = jnp.where(hit, -3e38, x)
    sel = sel.astype(BF16)

    for m_ref, l_ref, a_ref in ((ms_ref, ls_ref, as_ref), (mw_ref, lw_ref, aw_ref)):
        m_ref[...] = jnp.full(m_ref.shape, NEG_INF, F32)
        l_ref[...] = jnp.zeros(l_ref.shape, F32)
        a_ref[...] = jnp.zeros(a_ref.shape, F32)

    def tile_common(kt, src_ref):
        start = pl.multiple_of(kt * tq, tq)
        s = _dot(src_ref[pl.ds(start, tq), :LANES].astype(BF16), q8t) + bias_ref[jnp.minimum(qi - kt, 2)]
        dist = (qi - kt) * tq + t_loc - k_loc
        return start, s, dist

    def sel_tile(kt, carry):
        start, s, dist = tile_common(kt, sel_ref)
        expand = (_lane_iota((tq, n_sel)) == kt * (tq // SEL_BLOCK) + _row_iota((tq, n_sel)) // SEL_BLOCK)
        chosen = _dot(jnp.where(expand, 1.0, 0.0).astype(BF16), sel) > 0.5
        ok = [chosen[:, g * tq:(g + 1) * tq] & (dist >= 0) for g in range(NSA_GROUPS)]
        allowed = jnp.concatenate([ok[0]] * NSA_HPG + [ok[1]] * NSA_HPG, axis=1)
        _flash_step_cols(s, allowed, vst_ref[:, pl.ds(start, tq)], ms_ref, ls_ref, as_ref)
        return carry

    def win_tile(kt, carry):
        start, s, dist = tile_common(kt, win_ref)
        ok = (dist >= 0) & (dist < WINDOW)
        _flash_step_cols(s, heads_on_lanes(ok), vwt_ref[:, pl.ds(start, tq)], mw_ref, lw_ref, aw_ref)
        return carry

    lax.fori_loop(0, qi + 1, sel_tile, 0)
    lax.fori_loop(jnp.maximum(qi - WINDOW // tq, 0), qi + 1, win_tile, 0)

    o_s = as_ref[...] / ls_ref[...]
    o_w = aw_ref[...] / lw_ref[...]
    gate_t = gate_ref[...].T
    outs = []
    for hd in range(NSA_HEADS):
        sl = slice(hd * tq, (hd + 1) * tq)
        outs.append(gate_t[3 * hd:3 * hd + 1] * o_c[:, sl] + gate_t[3 * hd + 1:3 * hd + 2] * o_s[:, sl]
                    + gate_t[3 * hd + 2:3 * hd + 3] * o_w[:, sl])
    for v in range(NSA_HPG):
        o_ref[:, v * LANES:(v + 1) * LANES] = jnp.where(low, outs[v], outs[NSA_HPG + v]).T.astype(o_ref.dtype)


def _nsa_prompt(qn, gate, kc, rows, win, rel_bias):
    bsz, t, _ = qn.shape
    n_cmp = t // CMP_BLOCK
    n_sel = t // SEL_BLOCK
    kern = functools.partial(_nsa_prompt_kernel, n_sel=n_sel, n_pick=min(TOP_N, n_sel))
    stat = lambda r: pltpu.VMEM((r, _NROW), F32)
    return pl.pallas_call(
        kern,
        grid=(bsz, t // _TQ),
        in_specs=[pl.BlockSpec(memory_space=pltpu.SMEM),
                  pl.BlockSpec((None, _TQ, _QN_W), lambda b, i: (b, i, 0)),
                  pl.BlockSpec((None, _TQ, LANES), lambda b, i: (b, i, 0)),
                  pl.BlockSpec((None, n_cmp, _CMP_W), lambda b, i: (b, 0, 0)),
                  pl.BlockSpec((None, t, _WIN_W), lambda b, i: (b, 0, 1)),
                  pl.BlockSpec((None, t, _WIN_W), lambda b, i: (b, 0, 0))],
        out_specs=pl.BlockSpec((None, _TQ, _QN_W), lambda b, i: (b, i, 0)),
        out_shape=jax.ShapeDtypeStruct((bsz, t, _QN_W), BF16),
        scratch_shapes=[pltpu.VMEM((3, _TQ, _NROW), F32), pltpu.VMEM((LANES, t), BF16), pltpu.VMEM((LANES, t), BF16),
                        stat(1), stat(1), stat(LANES), stat(1), stat(1), stat(LANES)],
        compiler_params=_cparams(("arbitrary", "arbitrary")),
        name="nsa_prompt",
    )(rel_bias, qn, gate, kc, rows, win)


_TM = 256
_MLA_TK = 512


def _mla_prompt_kernel(q_ref, k_ref, c_ref, wv_ref, o_ref, m_ref, l_ref, acc_ref):
    qi = pl.program_id(1)
    tq, tk = _TM, _MLA_TK
    m_ref[...] = jnp.full(m_ref.shape, NEG_INF, F32)
    l_ref[...] = jnp.zeros(l_ref.shape, F32)
    acc_ref[...] = jnp.zeros(acc_ref.shape, F32)
    t_loc = _row_iota((tq, tk))
    k_loc = _lane_iota((tq, tk))

    def tile(kt, carry):
        start = pl.multiple_of(kt * tk, tk)
        c = c_ref[pl.ds(start, tk), :]
        allowed = qi * tq - kt * tk + t_loc >= k_loc
        for hd in range(MLA_HEADS):
            lanes = slice(hd * LANES, (hd + 1) * LANES)
            s = _dot_t(q_ref[:, lanes], k_ref[pl.ds(start, tk), lanes])
            s = jnp.where(allowed, s, NEG_INF)
            m_old = m_ref[hd]
            m_new = jnp.maximum(m_old, jnp.max(s, axis=-1, keepdims=True))
            alpha = jnp.exp(m_old - m_new)
            p = jnp.exp(s - m_new)
            l_ref[hd] = alpha * l_ref[hd] + jnp.sum(p, axis=-1, keepdims=True)
            acc_ref[hd] = alpha * acc_ref[hd] + _dot(p.astype(BF16), c)
            m_ref[hd] = m_new
        return carry

    lax.fori_loop(0, (qi * tq + tq + tk - 1) // tk, tile, 0)
    outs = [_dot((acc_ref[hd] / l_ref[hd]).astype(BF16), wv_ref[hd]) for hd in range(MLA_HEADS)]
    o_ref[...] = jnp.concatenate(outs, axis=1).astype(o_ref.dtype)


def _mla_prompt(qm, kmla, cb, wv):
    bsz, t, _ = qm.shape
    ow = MLA_HEADS * MLA_V_DIM
    return pl.pallas_call(
        _mla_prompt_kernel,
        grid=(bsz, t // _TM),
        in_specs=[pl.BlockSpec((None, _TM, _MLA_PAD_W), lambda b, i: (b, i, 0)),
                  pl.BlockSpec((None, t, _MLA_PAD_W), lambda b, i: (b, 0, 0)),
                  pl.BlockSpec((None, t, MLA_KV_RANK), lambda b, i: (b, 0, 0)),
                  _const_spec(wv.shape)],
        out_specs=pl.BlockSpec((None, _TM, ow), lambda b, i: (b, i, 0)),
        out_shape=jax.ShapeDtypeStruct((bsz, t, ow), BF16),
        scratch_shapes=[pltpu.VMEM((MLA_HEADS, _TM, 1), F32), pltpu.VMEM((MLA_HEADS, _TM, 1), F32),
                        pltpu.VMEM((MLA_HEADS, _TM, MLA_KV_RANK), F32)],
        compiler_params=_cparams(("arbitrary", "arbitrary")),
        name="mla_prompt",
    )(qm, kmla, cb, wv)


_MEM_W = MEM_HEADS * MEM_HEAD_DIM


def _merge_kernel(x_ref, on_ref, om_ref, g1_ref, wmg_ref, wno_ref, wmo_ref, wout_ref,
                  g2_ref, wq_ref, qg_ref, x1_ref, qmem_ref):
    x = x_ref[...]
    h = _rms_rows(x, g1_ref[...]).astype(BF16)
    mg = jax.nn.sigmoid(_dot(h, wmg_ref[...]))
    a = _dot(on_ref[...], wno_ref[...])
    b = _dot(om_ref[...], wmo_ref[...])
    r = mg[:, :D_MODEL] * a + mg[:, D_MODEL:] * b
    x1 = x + _dot(r.astype(BF16), wout_ref[...])
    x1_ref[...] = x1
    q = _dot(_rms_rows(x1, g2_ref[...]).astype(BF16), wq_ref[...])
    scale = MEM_HEAD_DIM ** -0.5
    for hd in range(MEM_HEADS):
        blk = q[:, hd * LANES:(hd + 1) * LANES]
        qmem_ref[:, hd * LANES:(hd + 1) * LANES] = (_rms_rows(blk, qg_ref[...]) * scale).astype(BF16)


def _merge(x, o_n, o_m, g1, wmg, wno, wmo, wout, g2, wq, qg, tm):
    n = x.shape[0]
    row = lambda w: pl.BlockSpec((tm, w), lambda i: (i, 0))
    consts = [g1.reshape(1, -1), wmg, wno, wmo, wout, g2.reshape(1, -1), wq, qg.reshape(1, -1)]
    return pl.pallas_call(
        _merge_kernel,
        grid=(n // tm,),
        in_specs=[row(D_MODEL), row(_QN_W), row(MLA_HEADS * MLA_V_DIM)] + [_const_spec(c.shape) for c in consts],
        out_specs=[row(D_MODEL), row(_MEM_W)],
        out_shape=[jax.ShapeDtypeStruct((n, D_MODEL), F32), jax.ShapeDtypeStruct((n, _MEM_W), BF16)],
        compiler_params=_cparams(("arbitrary",)),
        name="merge",
    )(x, o_n, o_m, *consts)


def _mem_kv_kernel(m_ref, g_ref, w_ref, kg_ref, o_ref):
    kv = _dot(_rms_rows(m_ref[...], g_ref[...]).astype(BF16), w_ref[...])
    for hd in range(MEM_HEADS):
        o_ref[:, hd * LANES:(hd + 1) * LANES] = _rms_rows(kv[:, hd * LANES:(hd + 1) * LANES], kg_ref[...])
    o_ref[:, _MEM_W:] = kv[:, _MEM_W:]


def _mem_kv(mem, g, w, kg):
    n = mem.shape[0]
    tm = min(n, _TM)
    consts = [g.reshape(1, -1), w, kg.reshape(1, -1)]
    return pl.pallas_call(
        _mem_kv_kernel,
        grid=(n // tm,),
        in_specs=[pl.BlockSpec((tm, D_MODEL), lambda i: (i, 0))] + [_const_spec(c.shape) for c in consts],
        out_specs=pl.BlockSpec((tm, 2 * _MEM_W), lambda i: (i, 0)),
        out_shape=jax.ShapeDtypeStruct((n, 2 * _MEM_W), F32),
        compiler_params=_cparams(("arbitrary",)),
        name="mem_kv",
    )(mem, *consts)


def _mem_attn_kernel(q_ref, kv_ref, o_ref, *, n_seq):
    tq = q_ref.shape[0] // n_seq
    for s in range(n_seq):
        outs = []
        for hd in range(MEM_HEADS):
            q = q_ref[s * tq:(s + 1) * tq, hd * LANES:(hd + 1) * LANES]
            sc = _dot_t(q, kv_ref[s, :, hd * LANES:(hd + 1) * LANES].astype(BF16))
            e = jnp.exp(sc - jnp.max(sc, axis=-1, keepdims=True))
            p = e / jnp.sum(e, axis=-1, keepdims=True)
            outs.append(_dot(p.astype(BF16), kv_ref[s, :, _MEM_W + hd * LANES:_MEM_W + (hd + 1) * LANES].astype(BF16)))
        o_ref[s * tq:(s + 1) * tq, :] = jnp.concatenate(outs, axis=1).astype(o_ref.dtype)


def _mem_attn(q, kv, tq, n_seq):
    n = q.shape[0]
    per_kv = n // kv.shape[0]
    rows = tq * n_seq
    if n_seq == 1:
        kv_map = lambda i: (i * tq // per_kv, 0, 0)
    else:
        kv_map = lambda i: (i, 0, 0)
    return pl.pallas_call(
        functools.partial(_mem_attn_kernel, n_seq=n_seq),
        grid=(n // rows,),
        in_specs=[pl.BlockSpec((rows, _MEM_W), lambda i: (i, 0)),
                  pl.BlockSpec((n_seq,) + kv.shape[1:], kv_map)],
        out_specs=pl.BlockSpec((rows, _MEM_W), lambda i: (i, 0)),
        out_shape=jax.ShapeDtypeStruct((n, _MEM_W), BF16),
        compiler_params=_cparams(("arbitrary",)),
        name="mem_attn",
    )(q, kv)


_FF_CHUNK = 256
_CARRY_ROWS = SUBLANES


def _ffn_kernel(x1_ref, om_ref, wo_ref, g3_ref, wup_ref, cw_ref, cb_ref, wdn_ref, past_ref,
                y_ref, a_ref, carry_ref, *, seq_tiles, seq_len):
    i = pl.program_id(0)
    tm = x1_ref.shape[0]
    x2 = x1_ref[...] + _dot(om_ref[...], wo_ref[...])
    h = _rms_rows(x2, g3_ref[...]).astype(BF16)
    short = seq_len < tm
    row = _row_iota((tm, _FF_CHUNK))
    rin = row % seq_len if short else row
    first = (i % seq_tiles) == 0
    acc = jnp.zeros((tm, D_MODEL), F32)
    for c in range(D_FF // _FF_CHUNK):
        cols = slice(c * _FF_CHUNK, (c + 1) * _FF_CHUNK)
        a = _dot(h, wup_ref[:, cols])
        v = _dot(h, wup_ref[:, D_FF + c * _FF_CHUNK:D_FF + (c + 1) * _FF_CHUNK])
        if short:
            prev = past_ref[:, cols]
            a1 = jnp.where(rin == 0, pltpu.roll(prev, tm - 1, 0), pltpu.roll(a, 1, 0))
            a2 = jnp.where(rin < 2, prev, pltpu.roll(a, 2, 0))
            a_ref[:, cols] = a
        else:
            prev = jnp.where(first, 0.0, carry_ref[:, cols])
            prev = jnp.concatenate([prev] * (tm // _CARRY_ROWS), axis=0)
            a1 = jnp.where(rin == 0, pltpu.roll(prev, 1, 0), pltpu.roll(a, 1, 0))
            a2 = jnp.where(rin < 2, pltpu.roll(prev, 2, 0), pltpu.roll(a, 2, 0))
            carry_ref[:, cols] = a[tm - _CARRY_ROWS:, :]
            a_ref[:, cols] = a[tm - _CARRY_ROWS:, :]
        cv = cb_ref[:, cols] + cw_ref[0:1, cols] * a2 + cw_ref[1:2, cols] * a1 + cw_ref[2:3, cols] * a
        act = (cv * jax.nn.sigmoid(cv) * v).astype(BF16)
        acc = acc + _dot(act, wdn_ref[cols, :])
    y_ref[...] = x2 + acc


def _ffn(x1, o_mem, wo, g3, wup, cw, cb, wdn, past, tm, seq_len):
    n = x1.shape[0]
    short = seq_len < tm
    row = lambda w: pl.BlockSpec((tm, w), lambda i: (i, 0))
    consts = [wo, g3.reshape(1, -1), wup, cw, cb.reshape(1, -1), wdn]
    if short:
        past_spec, a_rows, a_spec = row(D_FF), n, row(D_FF)
    else:
        past = jnp.zeros((_CARRY_ROWS, D_FF), F32)
        past_spec = _const_spec(past.shape)
        a_rows = n // tm * _CARRY_ROWS
        a_spec = pl.BlockSpec((_CARRY_ROWS, D_FF), lambda i: (i, 0))
    kern = functools.partial(_ffn_kernel, seq_tiles=max(seq_len // tm, 1), seq_len=seq_len)
    return pl.pallas_call(
        kern,
        grid=(n // tm,),
        in_specs=[row(D_MODEL), row(_MEM_W)] + [_const_spec(c.shape) for c in consts] + [past_spec],
        out_specs=[row(D_MODEL), a_spec],
        out_shape=[jax.ShapeDtypeStruct((n, D_MODEL), F32), jax.ShapeDtypeStruct((a_rows, D_FF), F32)],
        scratch_shapes=[pltpu.VMEM((_CARRY_ROWS, D_FF), F32)],
        compiler_params=_cparams(("arbitrary",)),
        name="ffn",
    )(x1, o_mem, *consts, past)


def _paged_loop(pt_ref, n_seq, n_chunks, pages, n_bufs, page_copy, body):
    s = pl.program_id(0)
    ahead = n_bufs - 1
    total = n_seq * n_chunks

    def start(g):
        seq, chunk, slot = g // n_chunks, g % n_chunks, g % n_bufs
        for i in range(pages):
            page_copy(pt_ref[seq, chunk * pages + i], slot, i).start()

    @pl.when(s == 0)
    def _():
        for g in range(min(ahead, total)):
            start(g)

    def step(j, carry):
        g = s * n_chunks + j
        slot = g % n_bufs
        for i in range(pages):
            page_copy(0, slot, i).wait()

        @pl.when(g + ahead < total)
        def _():
            start(g + ahead)

        body(j, slot)
        return carry

    lax.fori_loop(0, n_chunks, step, 0)


def _seq_grid_spec(n_seq, in_specs, out_specs, scratch):
    return pltpu.PrefetchScalarGridSpec(num_scalar_prefetch=1, grid=(n_seq,), in_specs=in_specs,
                                        out_specs=out_specs, scratch_shapes=scratch)


_CMP_PAGES = 64
_CMP_BUFS = 2
_BLK_PER_PAGE = PAGE_SIZE // CMP_BLOCK
_TAIL_BLOCKS = SUBLANES


_PAIR_KEYS = 2 * PAGE_SIZE
_PAIR_BLOCKS = _PAIR_KEYS // CMP_BLOCK


def _compress_paged_kernel(pt_ref, cache_ref, tail_ref, perm_ref, w1_ref, b1_ref, w2_ref, kg_ref, o_ref,
                           buf_ref, xs_ref, sem_ref, *, n_seq, n_chunks, pages):
    def page_copy(page, slot, i):
        return pltpu.make_async_copy(
            cache_ref.at[page, pl.ds(0, _CMP_W), :],
            buf_ref.at[slot, :, pl.ds(i * PAGE_SIZE, PAGE_SIZE)], sem_ref.at[slot])

    nb = pages * _BLK_PER_PAGE

    def body(j, slot):
        for pp in range(pages // 2):
            cols = buf_ref[slot, :, pp * _PAIR_KEYS:(pp + 1) * _PAIR_KEYS]
            xp = _dot_t(perm_ref[...], cols.astype(BF16))
            xs_ref[pp] = xp.reshape(CMP_BLOCK, _PAIR_BLOCKS, _CMP_W)
        o_ref[pl.ds(pl.multiple_of(j * nb, nb), nb), :] = _compress_math(
            xs_ref, w1_ref, b1_ref, w2_ref, kg_ref, row_major=False)

    n_past = n_chunks * nb
    o_ref[n_past:n_past + _TAIL_BLOCKS, :] = _compress_math(tail_ref, w1_ref, b1_ref, w2_ref, kg_ref)
    _paged_loop(pt_ref, n_seq, n_chunks, pages, _CMP_BUFS, page_copy, body)


def _compress_paged(page_table, cache, tail, cw):
    n_seq, n_pages = page_table.shape
    pages = min(_CMP_PAGES, n_pages)
    assert pages % 2 == 0
    n_chunks = n_pages // pages
    n_out = n_pages * _BLK_PER_PAGE + _TAIL_BLOCKS
    key = np.arange(_PAIR_KEYS)
    perm = np.zeros((_PAIR_KEYS, _PAIR_KEYS), np.float32)
    perm[(key % CMP_BLOCK) * _PAIR_BLOCKS + key // CMP_BLOCK, key] = 1.0
    consts = [jnp.asarray(perm, BF16), cw['w1'], cw['b1'], cw['w2'], cw['kg']]
    cs = lambda c: pl.BlockSpec(c.shape, lambda s, pt: (0,) * c.ndim)
    kern = functools.partial(_compress_paged_kernel, n_seq=n_seq, n_chunks=n_chunks, pages=pages)
    return pl.pallas_call(
        kern,
        grid_spec=_seq_grid_spec(
            n_seq,
            [pl.BlockSpec(memory_space=pl.ANY),
             pl.BlockSpec((None, _TAIL_BLOCKS, CMP_BLOCK, _CMP_W), lambda s, pt: (s, 0, 0, 0))]
            + [cs(c) for c in consts],
            pl.BlockSpec((None, n_out, _CMP_W), lambda s, pt: (s, 0, 0)),
            [pltpu.VMEM((_CMP_BUFS, _CMP_W, pages * PAGE_SIZE), F32),
             pltpu.VMEM((pages // 2, CMP_BLOCK, _PAIR_BLOCKS, _CMP_W), F32),
             pltpu.SemaphoreType.DMA((_CMP_BUFS,))]),
        out_shape=jax.ShapeDtypeStruct((n_seq, n_out, _CMP_W), F32),
        compiler_params=_cparams(("arbitrary",)),
        name="compress_paged",
    )(page_table, cache, tail, *consts)


_SEL_PAGES = 16
_SEL_BUFS = 4


def _nsa_sample_kernel(pt_ref, rb_ref, q_ref, gate_ref, kc_ref, cache_ref, new_ref, wst_ref, wnew_ref,
                       o_ref, buf_ref, sem_ref, ms_ref, ls_ref, as_ref, mw_ref, lw_ref, aw_ref,
                       *, n_seq, n_chunks, pages, past_len, n_cmp_real, n_pick):
    tq = q_ref.shape[0]
    nrow = NSA_HEADS * tq
    kp = pages * PAGE_SIZE

    def page_copy(page, slot, i):
        return pltpu.make_async_copy(
            cache_ref.at[page, pl.ds(_WIN_W, _WIN_W), :],
            buf_ref.at[slot, :, pl.ds(i * PAGE_SIZE, PAGE_SIZE)], sem_ref.at[slot])

    q8 = _stack_heads(q_ref[...])

    def head_bias(dist):
        return jnp.concatenate([_bias_from_dist(dist, rb_ref, hd) for hd in range(NSA_HEADS)], axis=0)

    def rep_heads(x):
        return jnp.concatenate([x] * NSA_HEADS, axis=0)

    kc = kc_ref[...]
    n_cmp = kc.shape[0]
    n_sel = n_cmp // (SEL_BLOCK // CMP_BLOCK)
    q_pos_c = past_len + _row_iota((tq, n_cmp))
    blk_c = _lane_iota((tq, n_cmp))
    dist_c = q_pos_c - (blk_c * CMP_BLOCK + (CMP_BLOCK - 1))
    ok_c = rep_heads((dist_c >= 0) & (blk_c < n_cmp_real))
    s_c = jnp.where(ok_c, _dot_t(q8, kc[:, :LANES].astype(BF16)) + head_bias(dist_c), NEG_INF)
    e_c = jnp.exp(s_c - jnp.max(s_c, axis=-1, keepdims=True))
    p_c = jnp.where(ok_c, e_c / jnp.sum(e_c, axis=-1, keepdims=True), 0.0)
    o_c = _dot(p_c.astype(BF16), kc[:, LANES:].astype(BF16))

    pair = (_row_iota((n_cmp, n_sel)) // (SEL_BLOCK // CMP_BLOCK) == _lane_iota((n_cmp, n_sel)))
    pair = jnp.where(pair, 1.0, 0.0).astype(BF16)
    imp_n = []
    for g in range(NSA_GROUPS):
        acc_g = p_c[(g * NSA_HPG) * tq:(g * NSA_HPG + 1) * tq]
        for hl in range(1, NSA_HPG):
            acc_g = acc_g + p_c[(g * NSA_HPG + hl) * tq:(g * NSA_HPG + hl + 1) * tq]
        imp_n.append(acc_g)
    x = _exact_dot01(jnp.concatenate(imp_n, axis=0), pair)
    blk = _lane_iota(x.shape)
    cur = (past_len + _row_iota(x.shape) % tq) // SEL_BLOCK
    forced = (blk == 0) | (blk == cur) | (blk == cur - 1)
    x = jnp.where(forced, FORCED_SCORE, jnp.where(blk <= cur, x, -1.0))
    picked = []
    for _ in range(n_pick):
        m = jnp.max(x, axis=-1, keepdims=True)
        idx = jnp.min(jnp.where(x == m, blk, n_sel), axis=-1, keepdims=True)
        picked.append(idx)
        x = jnp.where(blk == idx, -3e38, x)

    for m_ref, l_ref, a_ref in ((ms_ref, ls_ref, as_ref), (mw_ref, lw_ref, aw_ref)):
        m_ref[...] = jnp.full(m_ref.shape, NEG_INF, F32)
        l_ref[...] = jnp.zeros(l_ref.shape, F32)
        a_ref[...] = jnp.zeros(a_ref.shape, F32)

    def picked_mask(key_blk):
        key2 = jnp.concatenate([key_blk] * NSA_GROUPS, axis=0)
        hit = jnp.zeros(key2.shape, F32)
        for idx in picked:
            hit = jnp.where(key2 == idx, 1.0, hit)
        hit = hit > 0.5
        return jnp.concatenate([hit[:tq]] * NSA_HPG + [hit[tq:]] * NSA_HPG, axis=0)

    wst = wst_ref[...]
    w_s = wst.shape[1]
    dist_w = w_s + _row_iota((tq, w_s)) - _lane_iota((tq, w_s))
    _flash_step(_dot(q8, wst[:LANES].astype(BF16)) + head_bias(dist_w),
                rep_heads((dist_w >= 0) & (dist_w < WINDOW)), wst[LANES:].astype(BF16), mw_ref, lw_ref, aw_ref,
                v_feature_major=True)
    dist_n = _row_iota((tq, tq)) - _lane_iota((tq, tq))
    bias_n = head_bias(dist_n)
    ok_n = rep_heads(dist_n >= 0)
    wnew = wnew_ref[...]
    _flash_step(_dot_t(q8, wnew[:, :LANES].astype(BF16)) + bias_n, ok_n, wnew[:, LANES:].astype(BF16),
                mw_ref, lw_ref, aw_ref)

    new = new_ref[...]
    cur_new = (past_len + _lane_iota((tq, tq))) // SEL_BLOCK
    _flash_step(_dot_t(q8, new[:, :LANES].astype(BF16)) + bias_n, ok_n & picked_mask(cur_new),
                new[:, LANES:].astype(BF16), ms_ref, ls_ref, as_ref)

    far = head_bias(jnp.full((tq, PAGE_SIZE), REL_MAX_DIST, jnp.int32))
    near = head_bias(past_len + _row_iota((tq, PAGE_SIZE)) - (past_len - PAGE_SIZE + _lane_iota((tq, PAGE_SIZE))))

    def body(j, slot):
        cols = buf_ref[slot]
        last = jnp.where(j == n_chunks - 1, near, far)
        bias = jnp.concatenate([far] * (pages - 1) + [last], axis=1)
        key_blk = (j * kp + _lane_iota((tq, kp))) // SEL_BLOCK
        _flash_step(_dot(q8, cols[:LANES].astype(BF16)) + bias, picked_mask(key_blk),
                    cols[LANES:].astype(BF16), ms_ref, ls_ref, as_ref, v_feature_major=True)

    _paged_loop(pt_ref, n_seq, n_chunks, pages, _SEL_BUFS, page_copy, body)

    o_s = as_ref[...] / ls_ref[...]
    o_w = aw_ref[...] / lw_ref[...]
    gate = gate_ref[...]
    outs = []
    for hd in range(NSA_HEADS):
        sl = slice(hd * tq, (hd + 1) * tq)
        outs.append(gate[:, 3 * hd:3 * hd + 1] * o_c[sl] + gate[:, 3 * hd + 1:3 * hd + 2] * o_s[sl]
                    + gate[:, 3 * hd + 2:3 * hd + 3] * o_w[sl])
    o_ref[...] = _unstack_heads(jnp.concatenate(outs, axis=0), tq).astype(o_ref.dtype)


def _nsa_sample(page_table, rel_bias, qn, gate, kc, cache, rows_new, win_state, win_new, past_len):
    n_seq, n_pages = page_table.shape
    tq = qn.shape[1]
    pages = min(_SEL_PAGES, n_pages)
    n_chunks = n_pages // pages
    n_cmp = kc.shape[1]
    n_cmp_real = -(-(past_len + tq) // SEL_BLOCK) * SEL_BLOCK // CMP_BLOCK
    n_sel_real = n_cmp_real // (SEL_BLOCK // CMP_BLOCK)
    nrow = NSA_HEADS * tq
    kern = functools.partial(_nsa_sample_kernel, n_seq=n_seq, n_chunks=n_chunks, pages=pages, past_len=past_len,
                             n_cmp_real=n_cmp_real, n_pick=min(TOP_N, n_sel_real))
    per_seq = lambda shp, last=0: pl.BlockSpec((None,) + shp, lambda s, pt: (s, 0, last))
    stat = lambda w: pltpu.VMEM((nrow, w), F32)
    return pl.pallas_call(
        kern,
        grid_spec=_seq_grid_spec(
            n_seq,
            [pl.BlockSpec(memory_space=pltpu.SMEM), per_seq((tq, _QN_W)), per_seq((tq, LANES)),
             per_seq((n_cmp, _CMP_W)), pl.BlockSpec(memory_space=pl.ANY),
             per_seq((tq, _WIN_W), 1), per_seq((_WIN_W, win_state.shape[2])), per_seq((tq, _WIN_W))],
            per_seq((tq, _QN_W)),
            [pltpu.VMEM((_SEL_BUFS, _WIN_W, pages * PAGE_SIZE), F32), pltpu.SemaphoreType.DMA((_SEL_BUFS,)),
             stat(1), stat(1), stat(LANES), stat(1), stat(1), stat(LANES)]),
        out_shape=jax.ShapeDtypeStruct((n_seq, tq, _QN_W), BF16),
        compiler_params=_cparams(("arbitrary",)),
        name="nsa_sample",
    )(page_table, rel_bias, qn, gate, kc, cache, rows_new, win_state, win_new)


_MLA_PAGES = 16
_MLA_BUFS = 3


def _mla_sample_kernel(pt_ref, q_ref, cache_ref, new_ref, tab_ref, tabn_ref, wknt_ref, wv_ref, gk_ref,
                       o_ref, buf_ref, sem_ref, m_ref, l_ref, acc_ref, *, n_seq, n_chunks, pages):
    tq = o_ref.shape[0]
    nrow = MLA_HEADS * tq
    kp = pages * PAGE_SIZE

    def page_copy(page, slot, i):
        return pltpu.make_async_copy(
            cache_ref.at[page], buf_ref.at[slot, :, pl.ds(i * PAGE_SIZE, PAGE_SIZE)], sem_ref.at[slot])

    q = q_ref[...].astype(F32)
    lane = _lane_iota(q.shape)
    gk = gk_ref[...]
    qg = q * gk
    nope = jnp.where(lane < MLA_NOPE_DIM, qg, 0.0)
    nope = nope + pltpu.roll(nope, MLA_NOPE_DIM, 1)
    qn = jnp.concatenate([nope] * (MLA_HEADS * MLA_NOPE_DIM // LANES), axis=1)
    own = _lane_iota(qn.shape) // MLA_NOPE_DIM == _row_iota(qn.shape) // tq
    q_abs = _dot(jnp.where(own, qn, 0.0).astype(BF16), wknt_ref[...])
    up =pltpu.roll(q, LANES - ROPE_HALF, 1)
    dn = pltpu.roll(q, ROPE_HALF, 1)
    in_x1 = (lane >= MLA_NOPE_DIM) & (lane < MLA_NOPE_DIM + ROPE_HALF)
    in_x2 = (lane >= MLA_NOPE_DIM + ROPE_HALF) & (lane < MLA_QK_DIM)
    q_hat = jnp.where(in_x1, up, jnp.where(in_x2, -dn, 0.0)) * gk
    q_cos = pltpu.roll(jnp.where(lane >= MLA_NOPE_DIM, qg, 0.0), LANES - MLA_NOPE_DIM, 1)
    q_sin = pltpu.roll(q_hat, LANES - MLA_ROPE_DIM, 1)
    ones_rows = jnp.where((_lane_iota((SUBLANES, LANES)) >= 2 * MLA_ROPE_DIM), 1.0, 0.0)
    lhs_r = jnp.concatenate([jnp.where(lane < MLA_ROPE_DIM, q_cos, jnp.where(lane < 2 * MLA_ROPE_DIM, q_sin, 0.0)),
                             ones_rows], axis=0).astype(BF16)
    lhs_c = jnp.concatenate([wknt_ref[...], q_abs.astype(BF16)], axis=0)
    n_kn = MLA_HEADS * MLA_NOPE_DIM

    m_ref[...] = jnp.full(m_ref.shape, NEG_INF, F32)
    l_ref[...] = jnp.zeros(l_ref.shape, F32)
    acc_ref[...] = jnp.zeros(acc_ref.shape, F32)

    def attend(cols, tab, allowed):
        nk = cols.shape[1]
        c = cols[:MLA_KV_RANK].astype(BF16)
        kr = cols[MLA_KV_RANK:]
        sq = kr * kr
        hi = sq.astype(BF16).astype(F32)
        rhs = (jnp.concatenate([kr, kr, hi, sq - hi], axis=0) * tab).astype(BF16)
        sr = _dot(lhs_r, rhs)
        both = _dot(lhs_c, c)
        inv = []
        for hd in range(MLA_HEADS):
            blk = both[hd * MLA_NOPE_DIM:(hd + 1) * MLA_NOPE_DIM]
            ms = (jnp.sum(blk * blk, axis=0, keepdims=True) + sr[nrow:nrow + 1]) * (1.0 / MLA_QK_DIM)
            inv.append(jnp.broadcast_to(lax.rsqrt(ms + NORM_EPS), (tq, nk)))
        s = (both[n_kn:] + sr[:nrow]) * jnp.concatenate(inv, axis=0)
        if allowed is not None:
            s = jnp.where(allowed, s, NEG_INF)
        m_old = m_ref[...]
        m_new = jnp.maximum(m_old, jnp.max(s, axis=-1, keepdims=True))
        alpha = jnp.exp(m_old - m_new)
        p = jnp.exp(s - m_new)
        l_ref[...] = alpha * l_ref[...] + jnp.sum(p, axis=-1, keepdims=True)
        acc_ref[...] = alpha * acc_ref[...] + _dot_t(p.astype(BF16), c)
        m_ref[...] = m_new

    causal = _row_iota((tq, tq)) >= _lane_iota((tq, tq))
    attend(new_ref[...], tabn_ref[...], jnp.concatenate([causal] * MLA_HEADS, axis=0))

    def body(j, slot):
        attend(buf_ref[slot], tab_ref[:, pl.ds(pl.multiple_of(j * kp, kp), kp)], None)

    _paged_loop(pt_ref, n_seq, n_chunks, pages, _MLA_BUFS, page_copy, body)

    o = acc_ref[...] / l_ref[...]
    outs = [_dot(o[hd * tq:(hd + 1) * tq].astype(BF16), wv_ref[hd]) for hd in range(MLA_HEADS)]
    o_ref[...] = jnp.concatenate(outs, axis=1).astype(o_ref.dtype)


def _mla_key_tables(pos):
    inv = jnp.power(ROPE_BASE, -jnp.arange(ROPE_HALF, dtype=F32) / ROPE_HALF)
    ang = pos.astype(F32)[:, None] * inv[None, :]
    cos, sin = jnp.cos(ang).T, jnp.sin(ang).T
    return jnp.concatenate([cos, cos, sin, sin, jnp.ones((LANES - 2 * MLA_ROPE_DIM, pos.shape[0]), F32)], axis=0)


def _mla_sample(page_table, q_rows, cache, rows_new, tab_past, tab_new, wknt, wv, gk):
    n_seq, n_pages = page_table.shape
    tq = rows_new.shape[2]
    nrow = MLA_HEADS * tq
    pages = min(_MLA_PAGES, n_pages)
    n_chunks = n_pages // pages
    kern = functools.partial(_mla_sample_kernel, n_seq=n_seq, n_chunks=n_chunks, pages=pages)
    per_seq = lambda shp: pl.BlockSpec((None,) + shp, lambda s, pt: (s, 0, 0))
    cs = lambda c: pl.BlockSpec(c.shape, lambda s, pt: (0,) * c.ndim)
    ow = MLA_HEADS * MLA_V_DIM
    return pl.pallas_call(
        kern,
        grid_spec=_seq_grid_spec(
            n_seq,
            [per_seq((nrow, LANES)), pl.BlockSpec(memory_space=pl.ANY), per_seq((_MLA_ROW_W, tq)),
             cs(tab_past), cs(tab_new), cs(wknt), cs(wv), cs(gk)],
            per_seq((tq, ow)),
            [pltpu.VMEM((_MLA_BUFS, _MLA_ROW_W, pages * PAGE_SIZE), F32), pltpu.SemaphoreType.DMA((_MLA_BUFS,)),
             pltpu.VMEM((nrow, 1), F32), pltpu.VMEM((nrow, 1), F32), pltpu.VMEM((nrow, MLA_KV_RANK), F32)]),
        out_shape=jax.ShapeDtypeStruct((n_seq, tq, ow), BF16),
        compiler_params=_cparams(("arbitrary",)),
        name="mla_sample",
    )(page_table, q_rows, cache, rows_new, tab_past, tab_new, wknt, wv, gk)


_MEM_SEQS_PER_STEP = 8


def _token_tile(n):
    return min(_TM, n)


def _after_mixers(x, o_n, o_m, mem_kv, mem_tokens, mem_seqs, lw, past, seq_len):
    tm = _token_tile(x.shape[0])
    x1, qmem = _merge(x, o_n, o_m, lw['g1'], lw['wmg'], lw['wno'], lw['wmo'], lw['wout'],
                      lw['g2'], lw['mem_wq'], lw['mem_qg'], tm)
    om = _mem_attn(qmem, mem_kv, mem_tokens, mem_seqs)
    return _ffn(x1, om, lw['mem_wo'], lw['g3'], lw['wup'], lw['conv_w'], lw['conv_b'], lw['wdn'],
                past, tm, seq_len)


def kernel(x_prompt, x_sample, cache_nsa, cache_mla, state_win, cache_mem, state_conv, page_table,
           mem_prompt, rel_bias, norm1_g, w_in, nsa_q_norm, nsa_k_norm, nsa_cmp_w1, nsa_cmp_b1,
           nsa_cmp_w2, nsa_w_o, mla_q_norm, mla_kv_norm, mla_w_uq, mla_w_ukv, mla_qk_q_norm,
           mla_qk_k_norm, mla_w_o, w_out, norm2_g, mem_norm_g, mem_w_q, mem_w_kv, mem_q_norm,
           mem_k_norm, mem_w_o, norm3_g, ffn_w_up, ffn_conv_w, ffn_conv_b, ffn_w_down):
    depth = w_in.shape[0]
    bsz, t, _ = x_prompt.shape
    n_seq, tq, _ = x_sample.shape
    n_pool = cache_nsa.shape[1]
    n_pages = page_table.shape[1]
    past_len = n_pages * PAGE_SIZE
    w_s = state_win.shape[2]
    mem_len = mem_prompt.shape[1]
    assert t % _MLA_TK == 0 and _MLA_TK % _TM == 0 and past_len % SEL_BLOCK == 0 and tq <= SEL_BLOCK and tq % SUBLANES == 0
    layer = lambda a, l: a.reshape(a.shape[1:]) if depth == 1 else a[l]

    xp = x_prompt.reshape(bsz * t, D_MODEL)
    xs = x_sample.reshape(n_seq * tq, D_MODEL)
    tm_s = _token_tile(n_seq * tq)
    tabs_p = _rope_tables(jnp.arange(t, dtype=jnp.int32))
    pos_s = past_len + jnp.arange(tq, dtype=jnp.int32)
    tabs_s = tuple(jnp.tile(tb, (tm_s // tq, 1)) for tb in _rope_tables(pos_s))
    key_tab_past = _mla_key_tables(jnp.arange(past_len, dtype=jnp.int32))
    key_tab_new = _mla_key_tables(pos_s)
    cache_nsa_fm = jnp.transpose(cache_nsa, (0, 1, 3, 4, 5, 2)).reshape(depth * n_pool, _ROWS_W, PAGE_SIZE)
    cache_mla_fm = jnp.transpose(cache_mla, (0, 1, 3, 2)).reshape(depth * n_pool, _MLA_ROW_W, PAGE_SIZE)
    mem_flat = mem_prompt.reshape(bsz * mem_len, D_MODEL)

    outs = [[] for _ in range(9)]
    for l in range(depth):
        pw = _proj_weights(w_in[l], nsa_q_norm[l], nsa_k_norm[l], mla_q_norm[l], mla_kv_norm[l],
                           mla_w_uq[l], mla_qk_q_norm[l], mla_w_ukv[l], mla_qk_k_norm[l])
        cw = _compress_weights(nsa_cmp_w1[l], nsa_cmp_b1[l], nsa_cmp_w2[l], nsa_k_norm[l, 0])
        wno = nsa_w_o[l].reshape(NSA_GROUPS, NSA_HPG, NSA_HEAD_DIM, D_MODEL)
        wno = jnp.transpose(wno, (1, 0, 2, 3)).reshape(_QN_W, D_MODEL)
        lw = dict(g1=norm1_g[l], wmg=w_in[l][:, -2 * D_MODEL:].astype(BF16), wno=wno.astype(BF16),
                  wmo=mla_w_o[l].astype(BF16), wout=w_out[l].astype(BF16), g2=norm2_g[l],
                  mem_wq=mem_w_q[l].astype(BF16), mem_qg=mem_q_norm[l], mem_wo=mem_w_o[l].astype(BF16),
                  g3=norm3_g[l], wup=ffn_w_up[l].astype(BF16), conv_w=ffn_conv_w[l], conv_b=ffn_conv_b[l],
                  wdn=ffn_w_down[l].astype(BF16))

        pr = _project(xp, tabs_p, t // _TM, norm1_g[l], pw, _TM)
        nb_p = bsz * t // CMP_BLOCK
        kc_p = _compress_rows(pr['rows'].reshape(nb_p, CMP_BLOCK, _ROWS_W), cw, min(LANES, nb_p))
        per_b = lambda a: a.reshape(bsz, t, a.shape[-1])
        o_n = _nsa_prompt(per_b(pr['qn']), per_b(pr['gate']), kc_p.reshape(bsz, t // CMP_BLOCK, _CMP_W),
                          per_b(pr['rows']), per_b(pr['win']), rel_bias)
        o_m = _mla_prompt(per_b(pr['qm']), per_b(pr['kmla']), per_b(pr['cb']), pw['wv'])
        mkv = _mem_kv(mem_flat, mem_norm_g[l], mem_w_kv[l].astype(BF16), mem_k_norm[l])
        xp, a_tail = _after_mixers(xp, o_n.reshape(bsz * t, -1), o_m.reshape(bsz * t, -1),
                                   mkv.reshape(bsz, mem_len, 2 * _MEM_W), _TM, 1, lw, None, t)
        outs[0].append(pr['rows'].reshape(bsz, t, 4, NSA_GROUPS, NSA_HEAD_DIM))
        outs[1].append(pr['mrow'].reshape(bsz, t, _MLA_ROW_W))
        outs[2].append(pr['win'].reshape(bsz, t, 2, NSA_GROUPS, NSA_HEAD_DIM)[:, -min(WINDOW, t):])
        outs[3].append(mkv.reshape(bsz, mem_len, 2, MEM_HEADS, MEM_HEAD_DIM))
        outs[4].append(a_tail.reshape(bsz, t // _TM, _CARRY_ROWS, D_FF)[:, -1, _CARRY_ROWS - (CONV_W - 1):])

        sr = _project(xs, tabs_s, 1, norm1_g[l], pw, tm_s)
        rows_s = sr['rows'].reshape(n_seq, tq, _ROWS_W)
        win_new = sr['win'].reshape(n_seq, tq, _WIN_W)
        pt_l = page_table + l * n_pool
        tail = jnp.pad(rows_s, ((0, 0), (0, _TAIL_BLOCKS * CMP_BLOCK - tq), (0, 0)))
        kc_s = _compress_paged(pt_l, cache_nsa_fm, tail.reshape(n_seq, _TAIL_BLOCKS, CMP_BLOCK, _ROWS_W), cw)
        win_old = jnp.transpose(layer(state_win, l), (0, 2, 3, 4, 1)).reshape(n_seq, _WIN_W, w_s)
        o_n = _nsa_sample(pt_l, rel_bias, sr['qn'].reshape(n_seq, tq, _QN_W), sr['gate'].reshape(n_seq, tq, LANES),
                          kc_s, cache_nsa_fm, rows_s, win_old, win_new, past_len)
        q_rows = jnp.transpose(sr['qm'].reshape(n_seq, tq, MLA_HEADS, MLA_HEAD_PAD), (0, 2, 1, 3))
        mrow_s = sr['mrow'].reshape(n_seq, tq, _MLA_ROW_W)
        o_m = _mla_sample(pt_l, q_rows.reshape(n_seq, MLA_HEADS * tq, MLA_HEAD_PAD), cache_mla_fm,
                          jnp.transpose(mrow_s, (0, 2, 1)), key_tab_past, key_tab_new,
                          pw['wkn_flat'].T, pw['wv'], pw['qkk'])
        past = jnp.pad(layer(state_conv, l), ((0, 0), (0, tq - (CONV_W - 1)), (0, 0)))
        xs, a_s = _after_mixers(xs, o_n.reshape(n_seq * tq, -1), o_m.reshape(n_seq * tq, -1),
                                layer(cache_mem, l).reshape(n_seq, mem_len, 2 * _MEM_W), tq,
                                min(_MEM_SEQS_PER_STEP, n_seq), lw, past.reshape(n_seq * tq, D_FF), tq)
        outs[5].append(rows_s.reshape(n_seq, tq, 4, NSA_GROUPS, NSA_HEAD_DIM))
        outs[6].append(mrow_s)
        win_all = jnp.concatenate([win_old, jnp.transpose(win_new, (0, 2, 1))], axis=2)[:, :, -w_s:]
        win_all = win_all.reshape(n_seq, 2, NSA_GROUPS, NSA_HEAD_DIM, w_s)
        outs[7].append(jnp.transpose(win_all, (0, 4, 1, 2, 3)))
        outs[8].append(a_s.reshape(n_seq, tq, D_FF)[:, tq - (CONV_W - 1):])

    return (xp.reshape(bsz, t, D_MODEL), xs.reshape(n_seq, tq, D_MODEL)) + tuple(jnp.stack(o) for o in outs)
```

```python
import functools
import math

import numpy as np
import jax
import jax.numpy as jnp
from jax import lax
from jax.experimental import pallas as pl
from jax.experimental.pallas import tpu as pltpu

D_MODEL = 1024
PAGE_SIZE = 128
NSA_HEADS = 8
NSA_GROUPS = 2
NSA_HPG = NSA_HEADS // NSA_GROUPS
NSA_HEAD_DIM = 64
CMP_BLOCK = 32
SEL_BLOCK = 64
TOP_N = 16
WINDOW = 512
MLA_HEADS = 8
MLA_Q_RANK = 384
MLA_KV_RANK = 256
MLA_NOPE_DIM = 64
MLA_ROPE_DIM = 32
MLA_QK_DIM = MLA_NOPE_DIM + MLA_ROPE_DIM
MLA_V_DIM = 64
ROPE_BASE = 10000.0
MEM_HEADS = 4
MEM_HEAD_DIM = 128
D_FF = 2816
CONV_W = 3
REL_BUCKETS = 32
REL_MAX_DIST = 128
NORM_EPS = 1e-6
NEG_INF = -1e30
FORCED_SCORE = 1e9

LANES = 128
SUBLANES = 8
VMEM_LIMIT = 56 * 1024 * 1024

HALF = LANES // 2
MLA_HEAD_PAD = LANES
ROPE_HALF = MLA_ROPE_DIM // 2

F32 = jnp.float32
BF16 = jnp.bfloat16


def _cparams(sem):
    return pltpu.CompilerParams(dimension_semantics=sem, vmem_limit_bytes=VMEM_LIMIT)


def _const_spec(shape):
    nd = len(shape)
    return pl.BlockSpec(shape, lambda *_: (0,) * nd)


def _dot(a, b):
    return jnp.dot(a, b, preferred_element_type=F32)


def _dot_t(a, b):
    return lax.dot_general(a, b, (((1,), (1,)), ((), ())), preferred_element_type=F32)


def _lane_iota(shape):
    return lax.broadcasted_iota(jnp.int32, shape, len(shape) - 1)


def _row_iota(shape):
    return lax.broadcasted_iota(jnp.int32, shape, len(shape) - 2)


def _rms_rows(x, gain):
    ms = jnp.mean(x * x, axis=-1, keepdims=True)
    return x * lax.rsqrt(ms + NORM_EPS) * gain


def _rms_half_lanes(x, gain):
    low = _lane_iota(x.shape) < HALF
    x2 = x * x
    s_lo = jnp.sum(jnp.where(low, x2, 0.0), axis=-1, keepdims=True)
    s_hi = jnp.sum(jnp.where(low, 0.0, x2), axis=-1, keepdims=True)
    ms = jnp.where(low, s_lo, s_hi) * (1.0 / HALF)
    return x * lax.rsqrt(ms + NORM_EPS) * gain


def _rope_block(x, c, s1, s2):
    up = pltpu.roll(x, LANES - ROPE_HALF, 1)
    dn = pltpu.roll(x, ROPE_HALF, 1)
    return x * c + up * s1 + dn * s2


def _rel_bucket_upper():
    exact = REL_BUCKETS // 2
    ub = []
    for b in range(REL_BUCKETS - 1):
        if b < exact:
            ub.append(b)
        else:
            d = b
            while True:
                nxt = d + 1
                lg = exact + int(math.log(nxt / exact) / math.log(REL_MAX_DIST / exact) * (REL_BUCKETS - exact))
                if min(lg, REL_BUCKETS - 1) > b:
                    break
                d = nxt
            ub.append(d)
    return ub


_BUCKET_UB = _rel_bucket_upper()


def _bias_from_dist(dist, rb_ref, head):
    out = jnp.full(dist.shape, rb_ref[REL_BUCKETS - 1, head], F32)
    for b in range(REL_BUCKETS - 2, -1, -1):
        out = jnp.where(dist <= _BUCKET_UB[b], rb_ref[b, head], out)
    return out


_QN_W = NSA_HEADS * NSA_HEAD_DIM
_KV_W = 6 * NSA_GROUPS * NSA_HEAD_DIM
_PROJ_W = _QN_W + _KV_W + MLA_Q_RANK + MLA_KV_RANK + 2 * LANES
_O_KV = _QN_W
_O_CQ = _O_KV + _KV_W
_O_CKV = _O_CQ + MLA_Q_RANK
_O_KR = _O_CKV + MLA_KV_RANK
_O_G = _O_KR + LANES
_MLA_PAD_W = MLA_HEADS * MLA_HEAD_PAD
_ROWS_W = 4 * NSA_GROUPS * NSA_HEAD_DIM
_WIN_W = 2 * NSA_GROUPS * NSA_HEAD_DIM
_MLA_ROW_W = MLA_KV_RANK + MLA_ROPE_DIM


def _proj_kernel(x_ref, tc_ref, ts1_ref, ts2_ref, g1_ref, w_ref, qg_ref, ksg_ref, kwg_ref,
                 cqg_ref, ckvg_ref, wuq_ref, qkq_ref, wkn_ref, qkk_ref,
                 qn_ref, rows_ref, win_ref, gate_ref, qm_ref, mrow_ref, kmla_ref, cb_ref):
    x = x_ref[...]
    h = _rms_rows(x, g1_ref[...]).astype(BF16)
    y = _dot(h, w_ref[...])

    nsa_scale = NSA_HEAD_DIM ** -0.5
    for v in range(_QN_W // LANES):
        blk = y[:, v * LANES:(v + 1) * LANES]
        qn_ref[:, v * LANES:(v + 1) * LANES] = (_rms_half_lanes(blk, qg_ref[...]) * nsa_scale).astype(BF16)

    kv = lambda slot: y[:, _O_KV + slot * LANES:_O_KV + (slot + 1) * LANES]
    rows_ref[:, 0:LANES] = kv(0)
    rows_ref[:, LANES:2 * LANES] = kv(1)
    rows_ref[:, 2 * LANES:3 * LANES] = _rms_half_lanes(kv(2), ksg_ref[...])
    rows_ref[:, 3 * LANES:4 * LANES] = kv(3)
    win_ref[:, 0:LANES] = _rms_half_lanes(kv(4), kwg_ref[...])
    win_ref[:, LANES:2 * LANES] = kv(5)
    gate_ref[...] = jax.nn.sigmoid(y[:, _O_G:_O_G + LANES])

    tc, ts1, ts2 = tc_ref[...], ts1_ref[...], ts2_ref[...]
    mla_scale = MLA_QK_DIM ** -0.5
    inv_qk = 1.0 / MLA_QK_DIM

    cq = _rms_rows(y[:, _O_CQ:_O_CQ + MLA_Q_RANK], cqg_ref[...]).astype(BF16)
    qraw = _dot(cq, wuq_ref[...])
    for hd in range(MLA_HEADS):
        blk = qraw[:, hd * LANES:(hd + 1) * LANES]
        ms = jnp.sum(blk * blk, axis=-1, keepdims=True) * inv_qk
        blk = blk * lax.rsqrt(ms + NORM_EPS) * qkq_ref[...]
        qm_ref[:, hd * LANES:(hd + 1) * LANES] = (_rope_block(blk, tc, ts1, ts2) * mla_scale).astype(BF16)

    c = _rms_rows(y[:, _O_CKV:_O_CKV + MLA_KV_RANK], ckvg_ref[...])
    krv = y[:, _O_KR:_O_KR + LANES]
    mrow_ref[:, 0:MLA_KV_RANK] = c
    mrow_ref[:, MLA_KV_RANK:_MLA_ROW_W] = krv[:, 0:MLA_ROPE_DIM]
    cbf = c.astype(BF16)
    cb_ref[...] = cbf

    kr_at = pltpu.roll(jnp.where(_lane_iota(krv.shape) < MLA_ROPE_DIM, krv, 0.0), MLA_NOPE_DIM, 1)
    ss_r = jnp.sum(kr_at * kr_at, axis=-1, keepdims=True)
    kr_rot = _rope_block(kr_at * qkk_ref[...], tc, ts1, ts2)
    kn = _dot(cbf, wkn_ref[...])
    for hd in range(MLA_HEADS):
        blk = kn[:, hd * LANES:(hd + 1) * LANES]
        ms = (jnp.sum(blk * blk, axis=-1, keepdims=True) + ss_r) * inv_qk
        kmla_ref[:, hd * LANES:(hd + 1) * LANES] = (
            (blk * qkk_ref[...] + kr_rot) * lax.rsqrt(ms + NORM_EPS)).astype(BF16)


def _rope_tables(pos):
    inv = jnp.power(ROPE_BASE, -jnp.arange(ROPE_HALF, dtype=F32) / ROPE_HALF)
    ang = pos.astype(F32)[:, None] * inv[None, :]
    cos, sin = jnp.cos(ang), jnp.sin(ang)
    n = pos.shape[0]
    z = lambda w: jnp.zeros((n, w), F32)
    pad = MLA_HEAD_PAD - MLA_QK_DIM
    tc = jnp.concatenate([jnp.ones((n, MLA_NOPE_DIM), F32), cos, cos, z(pad)], axis=1)
    ts1 = jnp.concatenate([z(MLA_NOPE_DIM), -sin, z(ROPE_HALF), z(pad)], axis=1)
    ts2 = jnp.concatenate([z(MLA_NOPE_DIM), z(ROPE_HALF), sin, z(pad)], axis=1)
    return tc, ts1, ts2


def _pad_heads(w, n_heads, width, used):
    k = w.shape[0]
    w = w.reshape(k, n_heads, width)[:, :, :used]
    return jnp.pad(w, ((0, 0), (0, 0), (0, MLA_HEAD_PAD - used))).reshape(k, n_heads * MLA_HEAD_PAD)


def _mla_gain_pad(g):
    return jnp.pad(g, (0, MLA_HEAD_PAD - MLA_QK_DIM)).reshape(1, MLA_HEAD_PAD)


def _proj_weights(w_in, nsa_q_norm, nsa_k_norm, mla_q_norm, mla_kv_norm, mla_w_uq, mla_qk_q_norm,
                  mla_w_ukv, mla_qk_k_norm):
    o = np.cumsum((0, _QN_W, _KV_W, 3 * NSA_HEADS, MLA_Q_RANK, MLA_KV_RANK, MLA_ROPE_DIM))
    qn = w_in[:, o[0]:o[1]].reshape(D_MODEL, NSA_GROUPS, NSA_HPG, NSA_HEAD_DIM)
    qn = jnp.transpose(qn, (0, 2, 1, 3)).reshape(D_MODEL, _QN_W)
    padc = lambda w: jnp.pad(w, ((0, 0), (0, LANES - w.shape[1])))
    w = jnp.concatenate([qn, w_in[:, o[1]:o[2]], w_in[:, o[3]:o[4]], w_in[:, o[4]:o[5]],
                         padc(w_in[:, o[5]:o[6]]), padc(w_in[:, o[2]:o[3]])], axis=1).astype(BF16)
    tile2 = lambda g: jnp.tile(g, NSA_GROUPS).reshape(1, LANES)
    wkn = mla_w_ukv.reshape(MLA_KV_RANK, MLA_HEADS, MLA_NOPE_DIM + MLA_V_DIM)
    wv = jnp.transpose(wkn[:, :, MLA_NOPE_DIM:], (1, 0, 2)).astype(BF16)
    wkn_flat = wkn[:, :, :MLA_NOPE_DIM].reshape(MLA_KV_RANK, MLA_HEADS * MLA_NOPE_DIM)
    return dict(
        w=w, qg=tile2(nsa_q_norm), ksg=tile2(nsa_k_norm[1]), kwg=tile2(nsa_k_norm[2]),
        cqg=mla_q_norm.reshape(1, -1), ckvg=mla_kv_norm.reshape(1, -1),
        wuq=_pad_heads(mla_w_uq, MLA_HEADS, MLA_QK_DIM, MLA_QK_DIM).astype(BF16),
        qkq=_mla_gain_pad(mla_qk_q_norm),
        wkn=_pad_heads(wkn_flat, MLA_HEADS, MLA_NOPE_DIM, MLA_NOPE_DIM).astype(BF16),
        qkk=_mla_gain_pad(mla_qk_k_norm), wv=wv, wkn_flat=wkn_flat.astype(BF16))


def _project(x, pos_tables, n_pos_tiles, g1, pw, tm):
    n = x.shape[0]
    grid = (n // tm,)
    row = lambda w: pl.BlockSpec((tm, w), lambda i: (i, 0))
    tab = pl.BlockSpec((tm, LANES), lambda i: (i % n_pos_tiles, 0))
    consts = [g1.reshape(1, -1), pw['w'], pw['qg'], pw['ksg'], pw['kwg'], pw['cqg'], pw['ckvg'],
              pw['wuq'], pw['qkq'], pw['wkn'], pw['qkk']]
    outs = pl.pallas_call(
        _proj_kernel,
        grid=grid,
        in_specs=[row(D_MODEL), tab, tab, tab] + [_const_spec(c.shape) for c in consts],
        out_specs=[row(_QN_W), row(_ROWS_W), row(_WIN_W), row(LANES), row(_MLA_PAD_W),
                   row(_MLA_ROW_W), row(_MLA_PAD_W), row(MLA_KV_RANK)],
        out_shape=[jax.ShapeDtypeStruct((n, _QN_W), BF16), jax.ShapeDtypeStruct((n, _ROWS_W), F32),
                   jax.ShapeDtypeStruct((n, _WIN_W), F32), jax.ShapeDtypeStruct((n, LANES), F32),
                   jax.ShapeDtypeStruct((n, _MLA_PAD_W), BF16), jax.ShapeDtypeStruct((n, _MLA_ROW_W), F32),
                   jax.ShapeDtypeStruct((n, _MLA_PAD_W), BF16), jax.ShapeDtypeStruct((n, MLA_KV_RANK), BF16)],
        compiler_params=_cparams(("arbitrary",)),
        name="proj",
    )(x, *pos_tables, *consts)
    return dict(zip(('qn', 'rows', 'win', 'gate', 'qm', 'mrow', 'kmla', 'cb'), outs))


_CMP_W = 2 * NSA_GROUPS * NSA_HEAD_DIM


def _compress_math(x_ref, w1_ref, b1_ref, w2_ref, kg_ref, row_major=True):
    if row_major:
        rows_at = lambda r: x_ref[:, r, :]
    else:
        nb = x_ref.shape[0] * x_ref.shape[2]
        rows_at = lambda r: x_ref[:, r, :, :].reshape(nb, x_ref.shape[3])
    acc = _dot(rows_at(0).astype(BF16), w1_ref[0])
    for r in range(1, CMP_BLOCK):
        acc = acc + _dot(rows_at(r).astype(BF16), w1_ref[r])
    y = acc + b1_ref[...]
    y = y * jax.nn.sigmoid(y)
    z = _dot(y.astype(BF16), w2_ref[...])
    return jnp.concatenate([_rms_half_lanes(z[:, :LANES], kg_ref[...]), z[:, LANES:]], axis=1)


def _compress_kernel(x_ref, w1_ref, b1_ref, w2_ref, kg_ref, o_ref):
    o_ref[...] = _compress_math(x_ref, w1_ref, b1_ref, w2_ref, kg_ref)


def _compress_weights(w1, b1, w2, kc_norm):
    eye = jnp.eye(NSA_GROUPS, dtype=F32)
    w1r = w1.reshape(2, CMP_BLOCK, NSA_HEAD_DIM, NSA_HEAD_DIM)
    w1e = jnp.einsum('crde,cx,gy->rcgdxye', w1r, jnp.eye(2, dtype=F32), eye)
    w1e = w1e.reshape(CMP_BLOCK, _CMP_W, _CMP_W).astype(BF16)
    w2e = jnp.einsum('cde,cx,gy->cgdxye', w2, jnp.eye(2, dtype=F32), eye).reshape(_CMP_W, _CMP_W).astype(BF16)
    b1e = jnp.broadcast_to(b1[:, None, :], (2, NSA_GROUPS, NSA_HEAD_DIM)).reshape(1, _CMP_W)
    return dict(w1=w1e, b1=b1e, w2=w2e, kg=jnp.tile(kc_norm, NSA_GROUPS).reshape(1, LANES))


def _compress_rows(rows, cw, nbt):
    nb = rows.shape[0]
    consts = [cw['w1'], cw['b1'], cw['w2'], cw['kg']]
    return pl.pallas_call(
        _compress_kernel,
        grid=(nb // nbt,),
        in_specs=[pl.BlockSpec((nbt, CMP_BLOCK, _CMP_W), lambda i: (i, 0, 0))]
        + [_const_spec(c.shape) for c in consts],
        out_specs=pl.BlockSpec((nbt, _CMP_W), lambda i: (i, 0)),
        out_shape=jax.ShapeDtypeStruct((nb, _CMP_W), F32),
        compiler_params=_cparams(("arbitrary",)),
        name="compress",
    )(rows, *consts)


_TQ = 128
_NROW = NSA_HEADS * _TQ


def _stack_heads(q):
    low = _lane_iota((q.shape[0], LANES)) < HALF
    zero = jnp.zeros((), q.dtype)
    blocks = [q[:, v * LANES:(v + 1) * LANES] for v in range(NSA_HPG)]
    return jnp.concatenate([jnp.where(low, b, zero) for b in blocks]
                           + [jnp.where(low, zero, b) for b in blocks], axis=0)


def _unstack_heads(o, tq):
    low = _lane_iota((tq, LANES)) < HALF
    return jnp.concatenate(
        [jnp.where(low, o[v * tq:(v + 1) * tq], o[(NSA_HPG + v) * tq:(NSA_HPG + v + 1) * tq])
         for v in range(NSA_HPG)], axis=1)


def _flash_step(s, allowed, v, m_ref, l_ref, acc_ref, v_feature_major=False):
    s = jnp.where(allowed, s, NEG_INF)
    m_old = m_ref[...]
    m_new = jnp.maximum(m_old, jnp.max(s, axis=-1, keepdims=True))
    alpha = jnp.exp(m_old - m_new)
    p = jnp.exp(s - m_new)
    l_ref[...] = alpha * l_ref[...] + jnp.sum(p, axis=-1, keepdims=True)
    pb = p.astype(BF16)
    acc_ref[...] = alpha * acc_ref[...] + (_dot_t(pb, v) if v_feature_major else _dot(pb, v))
    m_ref[...] = m_new


def _exact_dot01_rhs(m01, x):
    a = x.astype(BF16)
    r1 = x - a.astype(F32)
    b = r1.astype(BF16)
    c = (r1 - b.astype(F32)).astype(BF16)
    return _dot(m01, a) + _dot(m01, b) + _dot(m01, c)


def _flash_step_cols(s, allowed, vt, m_ref, l_ref, acc_ref):
    s = jnp.where(allowed, s, NEG_INF)
    m_old = m_ref[...]
    m_new = jnp.maximum(m_old, jnp.max(s, axis=0, keepdims=True))
    alpha = jnp.exp(m_old - m_new)
    p = jnp.exp(s - m_new)
    l_ref[...] = alpha * l_ref[...] + jnp.sum(p, axis=0, keepdims=True)
    acc_ref[...] = alpha * acc_ref[...] + _dot(vt, p.astype(BF16))
    m_ref[...] = m_new


def _nsa_prompt_kernel(rb_ref, q_ref, gate_ref, kc_ref, sel_ref, win_ref, o_ref,
                       bias_ref, vst_ref, vwt_ref, ms_ref, ls_ref, as_ref, mw_ref, lw_ref, aw_ref,
                       *, n_sel, n_pick):
    b = pl.program_id(0)
    qi = pl.program_id(1)
    tq = _TQ
    n_tiles = sel_ref.shape[0] // tq
    k_loc = _row_iota((tq, tq))
    t_loc = _lane_iota((tq, tq))
    heads_on_lanes = lambda x: jnp.concatenate([x] * NSA_HEADS, axis=1)

    @pl.when((b == 0) & (qi == 0))
    def _():
        for kind in range(3):
            dist = t_loc - k_loc + kind * tq
            for hd in range(NSA_HEADS):
                bias_ref[kind, :, hd * tq:(hd + 1) * tq] = _bias_from_dist(dist, rb_ref, hd)

    @pl.when(qi == 0)
    def _():
        for i in range(n_tiles):
            rows = slice(i * tq, (i + 1) * tq)
            vst_ref[:, rows] = sel_ref[rows, LANES:].T.astype(BF16)
            vwt_ref[:, rows] = win_ref[rows, LANES:].T.astype(BF16)

    qt = q_ref[...].astype(F32).T
    low = _row_iota((LANES, tq)) < HALF
    pairs = [qt[v * LANES:(v + 1) * LANES] for v in range(NSA_HPG)]
    q8t = jnp.concatenate([jnp.where(low, p, 0.0) for p in pairs]
                          + [jnp.where(low, 0.0, p) for p in pairs], axis=1).astype(BF16)

    kc = kc_ref[...]
    n_cmp = kc.shape[0]
    dist_c = (qi * tq + _lane_iota((n_cmp, tq))) - (_row_iota((n_cmp, tq)) * CMP_BLOCK + (CMP_BLOCK - 1))
    valid_c = heads_on_lanes(dist_c >= 0)
    bias_c = jnp.concatenate([_bias_from_dist(dist_c, rb_ref, hd) for hd in range(NSA_HEADS)], axis=1)
    s_c = jnp.where(valid_c, _dot(kc[:, :LANES].astype(BF16), q8t) + bias_c, NEG_INF)
    e_c = jnp.exp(s_c - jnp.max(s_c, axis=0, keepdims=True))
    p_c = jnp.where(valid_c, e_c / jnp.sum(e_c, axis=0, keepdims=True), 0.0)
    o_c = _dot(kc[:, LANES:].T.astype(BF16), p_c.astype(BF16))

    imp_n = []
    for g in range(NSA_GROUPS):
        acc_g = p_c[:, (g * NSA_HPG) * tq:(g * NSA_HPG + 1) * tq]
        for hl in range(1, NSA_HPG):
            acc_g = acc_g + p_c[:, (g * NSA_HPG + hl) * tq:(g * NSA_HPG + hl + 1) * tq]
        imp_n.append(acc_g)
    pair = (_lane_iota((n_sel, n_cmp)) // (SEL_BLOCK // CMP_BLOCK) == _row_iota((n_sel, n_cmp)))
    imp = _exact_dot01_rhs(jnp.where(pair, 1.0, 0.0).astype(BF16), jnp.concatenate(imp_n, axis=1))
    blk = _row_iota(imp.shape)
    cur = (qi * tq + _lane_iota(imp.shape) % tq) // SEL_BLOCK
    forced = (blk == 0) | (blk == cur) | (blk == cur - 1)
    x = jnp.where(forced, FORCED_SCORE, jnp.where(blk <= cur, imp, -1.0))
    sel = jnp.zeros(imp.shape, F32)
    for _ in range(n_pick):
        m = jnp.max(x, axis=0, keepdims=True)
        idx = jnp.min(jnp.where(x == m, blk, n_sel), axis=0, keepdims=True)
        hit = blk == idx
        sel = jnp.where(hit, 1.0, sel)
        x = jnp.where(hit, -3e38, x)
    sel = sel.astype(BF16)

    for m_ref, l_ref, a_ref in ((ms_ref, ls_ref, as_ref), (mw_ref, lw_ref, aw_ref)):
        m_ref[...] = jnp.full(m_ref.shape, NEG_INF, F32)
        l_ref[...] = jnp.zeros(l_ref.shape, F32)
        a_ref[...] = jnp.zeros(a_ref.shape, F32)

    def tile_common(kt, src_ref):
        start = pl.multiple_of(kt * tq, tq)
        s = _dot(src_ref[pl.ds(start, tq), :LANES].astype(BF16), q8t) + bias_ref[jnp.minimum(qi - kt, 2)]
        dist = (qi - kt) * tq + t_loc - k_loc
        return start, s, dist

    def sel_tile(kt, carry):
        start, s, dist = tile_common(kt, sel_ref)
        expand = (_lane_iota((tq, n_sel)) == kt * (tq // SEL_BLOCK) + _row_iota((tq, n_sel)) // SEL_BLOCK)
        chosen = _dot(jnp.where(expand, 1.0, 0.0).astype(BF16), sel) > 0.5
        ok = [chosen[:, g * tq:(g + 1) * tq] & (dist >= 0) for g in range(NSA_GROUPS)]
        allowed = jnp.concatenate([ok[0]] * NSA_HPG + [ok[1]] * NSA_HPG, axis=1)
        _flash_step_cols(s, allowed, vst_ref[:, pl.ds(start, tq)], ms_ref, ls_ref, as_ref)
        return carry

    def win_tile(kt, carry):
        start, s, dist = tile_common(kt, win_ref)
        ok = (dist >= 0) & (dist < WINDOW)
        _flash_step_cols(s, heads_on_lanes(ok), vwt_ref[:, pl.ds(start, tq)], mw_ref, lw_ref, aw_ref)
        return carry

    lax.fori_loop(0, qi + 1, sel_tile, 0)
    lax.fori_loop(jnp.maximum(qi - WINDOW // tq, 0), qi + 1, win_tile, 0)

    o_s = as_ref[...] / ls_ref[...]
    o_w = aw_ref[...] / lw_ref[...]
    gate_t = gate_ref[...].T
    outs = []
    for hd in range(NSA_HEADS):
        sl = slice(hd * tq, (hd + 1) * tq)
        outs.append(gate_t[3 * hd:3 * hd + 1] * o_c[:, sl] + gate_t[3 * hd + 1:3 * hd + 2] * o_s[:, sl]
                    + gate_t[3 * hd + 2:3 * hd + 3] * o_w[:, sl])
    for v in range(NSA_HPG):
        o_ref[:, v * LANES:(v + 1) * LANES] = jnp.where(low, outs[v], outs[NSA_HPG + v]).T.astype(o_ref.dtype)


def _nsa_prompt(qn, gate, kc, rows, win, rel_bias):
    bsz, t, _ = qn.shape
    n_cmp = t // CMP_BLOCK
    n_sel = t // SEL_BLOCK
    kern = functools.partial(_nsa_prompt_kernel, n_sel=n_sel, n_pick=min(TOP_N, n_sel))
    stat = lambda r: pltpu.VMEM((r, _NROW), F32)
    return pl.pallas_call(
        kern,
        grid=(bsz, t // _TQ),
        in_specs=[pl.BlockSpec(memory_space=pltpu.SMEM),
                  pl.BlockSpec((None, _TQ, _QN_W), lambda b, i: (b, i, 0)),
                  pl.BlockSpec((None, _TQ, LANES), lambda b, i: (b, i, 0)),
                  pl.BlockSpec((None, n_cmp, _CMP_W), lambda b, i: (b, 0, 0)),
                  pl.BlockSpec((None, t, _WIN_W), lambda b, i: (b, 0, 1)),
                  pl.BlockSpec((None, t, _WIN_W), lambda b, i: (b, 0, 0))],
        out_specs=pl.BlockSpec((None, _TQ, _QN_W), lambda b, i: (b, i, 0)),
        out_shape=jax.ShapeDtypeStruct((bsz, t, _QN_W), BF16),
        scratch_shapes=[pltpu.VMEM((3, _TQ, _NROW), F32), pltpu.VMEM((LANES, t), BF16), pltpu.VMEM((LANES, t), BF16),
                        stat(1), stat(1), stat(LANES), stat(1), stat(1), stat(LANES)],
        compiler_params=_cparams(("arbitrary", "arbitrary")),
        name="nsa_prompt",
    )(rel_bias, qn, gate, kc, rows, win)


_TM = 256
_MLA_TK = 512


def _mla_prompt_kernel(q_ref, k_ref, c_ref, wvt_ref, o_ref, ct_ref, m_ref, l_ref, acc_ref):
    qi = pl.program_id(1)
    tq, tk = _TM, _MLA_TK

    @pl.when(qi == 0)
    def _():
        for i in range(c_ref.shape[0] // LANES):
            rows = slice(i * LANES, (i + 1) * LANES)
            ct_ref[:, rows] = c_ref[rows, :].astype(F32).T.astype(BF16)

    qt = q_ref[...].astype(F32).T.astype(BF16)
    m_ref[...] = jnp.full(m_ref.shape, NEG_INF, F32)
    l_ref[...] = jnp.zeros(l_ref.shape, F32)
    acc_ref[...] = jnp.zeros(acc_ref.shape, F32)
    k_loc = _row_iota((tk, tq))
    t_loc = _lane_iota((tk, tq))

    def tile(kt, carry):
        start = pl.multiple_of(kt * tk, tk)
        ct = ct_ref[:, pl.ds(start, tk)]
        allowed = qi * tq - kt * tk + t_loc >= k_loc
        for hd in range(MLA_HEADS):
            lanes = slice(hd * LANES, (hd + 1) * LANES)
            s = jnp.where(allowed, _dot(k_ref[pl.ds(start, tk), lanes], qt[lanes]), NEG_INF)
            m_old = m_ref[hd]
            m_new = jnp.maximum(m_old, jnp.max(s, axis=0, keepdims=True))
            alpha = jnp.exp(m_old - m_new)
            p = jnp.exp(s - m_new)
            l_ref[hd] = alpha * l_ref[hd] + jnp.sum(p, axis=0, keepdims=True)
            acc_ref[hd] = alpha * acc_ref[hd] + _dot(ct, p.astype(BF16))
            m_ref[hd] = m_new
        return carry

    lax.fori_loop(0, (qi * tq + tq + tk - 1) // tk, tile, 0)
    outs = [_dot(wvt_ref[hd], (acc_ref[hd] / l_ref[hd]).astype(BF16)) for hd in range(MLA_HEADS)]
    o_ref[...] = jnp.concatenate(outs, axis=0).T.astype(o_ref.dtype)


def _mla_prompt(qm, kmla, cb, wv):
    bsz, t, _ = qm.shape
    ow = MLA_HEADS * MLA_V_DIM
    wvt = jnp.transpose(wv, (0, 2, 1))
    return pl.pallas_call(
        _mla_prompt_kernel,
        grid=(bsz, t // _TM),
        in_specs=[pl.BlockSpec((None, _TM, _MLA_PAD_W), lambda b, i: (b, i, 0)),
                  pl.BlockSpec((None, t, _MLA_PAD_W), lambda b, i: (b, 0, 0)),
                  pl.BlockSpec((None, t, MLA_KV_RANK), lambda b, i: (b, 0, 0)),
                  _const_spec(wvt.shape)],
        out_specs=pl.BlockSpec((None, _TM, ow), lambda b, i: (b, i, 0)),
        out_shape=jax.ShapeDtypeStruct((bsz, t, ow), BF16),
        scratch_shapes=[pltpu.VMEM((MLA_KV_RANK, t), BF16),
                        pltpu.VMEM((MLA_HEADS, 1, _TM), F32), pltpu.VMEM((MLA_HEADS, 1, _TM), F32),
                        pltpu.VMEM((MLA_HEADS, MLA_KV_RANK, _TM), F32)],
        compiler_params=_cparams(("arbitrary", "arbitrary")),
        name="mla_prompt",
    )(qm, kmla, cb, wvt)


_MEM_W = MEM_HEADS * MEM_HEAD_DIM


def _merge_kernel(x_ref, on_ref, om_ref, g1_ref, wmg_ref, wno_ref, wmo_ref, wout_ref,
                  g2_ref, wq_ref, qg_ref, x1_ref, qmem_ref):
    x = x_ref[...]
    h = _rms_rows(x, g1_ref[...]).astype(BF16)
    mg = jax.nn.sigmoid(_dot(h, wmg_ref[...]))
    a = _dot(on_ref[...], wno_ref[...])
    b = _dot(om_ref[...], wmo_ref[...])
    r = mg[:, :D_MODEL] * a + mg[:, D_MODEL:] * b
    x1 = x + _dot(r.astype(BF16), wout_ref[...])
    x1_ref[...] = x1
    q = _dot(_rms_rows(x1, g2_ref[...]).astype(BF16), wq_ref[...])
    scale = MEM_HEAD_DIM ** -0.5
    for hd in range(MEM_HEADS):
        blk = q[:, hd * LANES:(hd + 1) * LANES]
        qmem_ref[:, hd * LANES:(hd + 1) * LANES] = (_rms_rows(blk, qg_ref[...]) * scale).astype(BF16)


def _merge(x, o_n, o_m, g1, wmg, wno, wmo, wout, g2, wq, qg, tm):
    n = x.shape[0]
    row = lambda w: pl.BlockSpec((tm, w), lambda i: (i, 0))
    consts = [g1.reshape(1, -1), wmg, wno, wmo, wout, g2.reshape(1, -1), wq, qg.reshape(1, -1)]
    return pl.pallas_call(
        _merge_kernel,
        grid=(n // tm,),
        in_specs=[row(D_MODEL), row(_QN_W), row(MLA_HEADS * MLA_V_DIM)] + [_const_spec(c.shape) for c in consts],
        out_specs=[row(D_MODEL), row(_MEM_W)],
        out_shape=[jax.ShapeDtypeStruct((n, D_MODEL), F32), jax.ShapeDtypeStruct((n, _MEM_W), BF16)],
        compiler_params=_cparams(("arbitrary",)),
        name="merge",
    )(x, o_n, o_m, *consts)


def _mem_kv_kernel(m_ref, g_ref, w_ref, kg_ref, o_ref):
    kv = _dot(_rms_rows(m_ref[...], g_ref[...]).astype(BF16), w_ref[...])
    for hd in range(MEM_HEADS):
        o_ref[:, hd * LANES:(hd + 1) * LANES] = _rms_rows(kv[:, hd * LANES:(hd + 1) * LANES], kg_ref[...])
    o_ref[:, _MEM_W:] = kv[:, _MEM_W:]


def _mem_kv(mem, g, w, kg):
    n = mem.shape[0]
    tm = min(n, _TM)
    consts = [g.reshape(1, -1), w, kg.reshape(1, -1)]
    return pl.pallas_call(
        _mem_kv_kernel,
        grid=(n // tm,),
        in_specs=[pl.BlockSpec((tm, D_MODEL), lambda i: (i, 0))] + [_const_spec(c.shape) for c in consts],
        out_specs=pl.BlockSpec((tm, 2 * _MEM_W), lambda i: (i, 0)),
        out_shape=jax.ShapeDtypeStruct((n, 2 * _MEM_W), F32),
        compiler_params=_cparams(("arbitrary",)),
        name="mem_kv",
    )(mem, *consts)


def _mem_attn_kernel(q_ref, kv_ref, o_ref, *, n_seq):
    tq = q_ref.shape[0] // n_seq
    for s in range(n_seq):
        outs = []
        for hd in range(MEM_HEADS):
            q = q_ref[s * tq:(s + 1) * tq, hd * LANES:(hd + 1) * LANES]
            sc = _dot_t(q, kv_ref[s, :, hd * LANES:(hd + 1) * LANES].astype(BF16))
            e = jnp.exp(sc - jnp.max(sc, axis=-1, keepdims=True))
            p = e / jnp.sum(e, axis=-1, keepdims=True)
            outs.append(_dot(p.astype(BF16), kv_ref[s, :, _MEM_W + hd * LANES:_MEM_W + (hd + 1) * LANES].astype(BF16)))
        o_ref[s * tq:(s + 1) * tq, :] = jnp.concatenate(outs, axis=1).astype(o_ref.dtype)


def _mem_attn(q, kv, tq, n_seq):
    n = q.shape[0]
    per_kv = n // kv.shape[0]
    rows = tq * n_seq
    if n_seq == 1:
        kv_map = lambda i: (i * tq // per_kv, 0, 0)
    else:
        kv_map = lambda i: (i, 0, 0)
    return pl.pallas_call(
        functools.partial(_mem_attn_kernel, n_seq=n_seq),
        grid=(n // rows,),
        in_specs=[pl.BlockSpec((rows, _MEM_W), lambda i: (i, 0)),
                  pl.BlockSpec((n_seq,) + kv.shape[1:], kv_map)],
        out_specs=pl.BlockSpec((rows, _MEM_W), lambda i: (i, 0)),
        out_shape=jax.ShapeDtypeStruct((n, _MEM_W), BF16),
        compiler_params=_cparams(("arbitrary",)),
        name="mem_attn",
    )(q, kv)


_FF_CHUNK = 256
_CARRY_ROWS = SUBLANES


def _ffn_kernel(x1_ref, om_ref, wo_ref, g3_ref, wup_ref, cw_ref, cb_ref, wdn_ref, past_ref,
                y_ref, a_ref, carry_ref, *, seq_tiles, seq_len):
    i = pl.program_id(0)
    tm = x1_ref.shape[0]
    x2 = x1_ref[...] + _dot(om_ref[...], wo_ref[...])
    h = _rms_rows(x2, g3_ref[...]).astype(BF16)
    short = seq_len < tm
    row = _row_iota((tm, _FF_CHUNK))
    rin = row % seq_len if short else row
    first = (i % seq_tiles) == 0
    acc = jnp.zeros((tm, D_MODEL), F32)
    for c in range(D_FF // _FF_CHUNK):
        cols = slice(c * _FF_CHUNK, (c + 1) * _FF_CHUNK)
        a = _dot(h, wup_ref[:, cols])
        v = _dot(h, wup_ref[:, D_FF + c * _FF_CHUNK:D_FF + (c + 1) * _FF_CHUNK])
        if short:
            prev = past_ref[:, cols]
            a1 = jnp.where(rin == 0, pltpu.roll(prev, tm - 1, 0), pltpu.roll(a, 1, 0))
            a2 = jnp.where(rin < 2, prev, pltpu.roll(a, 2, 0))
            a_ref[:, cols] = a
        else:
            prev = jnp.where(first, 0.0, carry_ref[:, cols])
            prev = jnp.concatenate([prev] * (tm // _CARRY_ROWS), axis=0)
            a1 = jnp.where(rin == 0, pltpu.roll(prev, 1, 0), pltpu.roll(a, 1, 0))
            a2 = jnp.where(rin < 2, pltpu.roll(prev, 2, 0), pltpu.roll(a, 2, 0))
            carry_ref[:, cols] = a[tm - _CARRY_ROWS:, :]
            a_ref[:, cols] = a[tm - _CARRY_ROWS:, :]
        cv = cb_ref[:, cols] + cw_ref[0:1, cols] * a2 + cw_ref[1:2, cols] * a1 + cw_ref[2:3, cols] * a
        act = (cv * jax.nn.sigmoid(cv) * v).astype(BF16)
        acc = acc + _dot(act, wdn_ref[cols, :])
    y_ref[...] = x2 + acc


def _ffn(x1, o_mem, wo, g3, wup, cw, cb, wdn, past, tm, seq_len):
    n = x1.shape[0]
    short = seq_len < tm
    row = lambda w: pl.BlockSpec((tm, w), lambda i: (i, 0))
    consts = [wo, g3.reshape(1, -1), wup, cw, cb.reshape(1, -1), wdn]
    if short:
        past_spec, a_rows, a_spec = row(D_FF), n, row(D_FF)
    else:
        past = jnp.zeros((_CARRY_ROWS, D_FF), F32)
        past_spec = _const_spec(past.shape)
        a_rows = n // tm * _CARRY_ROWS
        a_spec = pl.BlockSpec((_CARRY_ROWS, D_FF), lambda i: (i, 0))
    kern = functools.partial(_ffn_kernel, seq_tiles=max(seq_len // tm, 1), seq_len=seq_len)
    return pl.pallas_call(
        kern,
        grid=(n // tm,),
        in_specs=[row(D_MODEL), row(_MEM_W)] + [_const_spec(c.shape) for c in consts] + [past_spec],
        out_specs=[row(D_MODEL), a_spec],
        out_shape=[jax.ShapeDtypeStruct((n, D_MODEL), F32), jax.ShapeDtypeStruct((a_rows, D_FF), F32)],
        scratch_shapes=[pltpu.VMEM((_CARRY_ROWS, D_FF), F32)],
        compiler_params=_cparams(("arbitrary",)),
        name="ffn",
    )(x1, o_mem, *consts, past)


def _paged_loop(pt_ref, n_seq, n_chunks, pages, n_bufs, page_copy, body):
    s = pl.program_id(0)
    ahead = n_bufs - 1
    total = n_seq * n_chunks

    def start(g):
        seq, chunk, slot = g // n_chunks, g % n_chunks, g % n_bufs
        for i in range(pages):
            page_copy(pt_ref[seq, chunk * pages + i], slot, i).start()

    @pl.when(s == 0)
    def _():
        for g in range(min(ahead, total)):
            start(g)

    def step(j, carry):
        g = s * n_chunks + j
        slot = g % n_bufs
        for i in range(pages):
            page_copy(0, slot, i).wait()

        @pl.when(g + ahead < total)
        def _():
            start(g + ahead)

        body(j, slot)
        return carry

    lax.fori_loop(0, n_chunks, step, 0)


def _seq_grid_spec(n_seq, in_specs, out_specs, scratch):
    return pltpu.PrefetchScalarGridSpec(num_scalar_prefetch=1, grid=(n_seq,), in_specs=in_specs,
                                        out_specs=out_specs, scratch_shapes=scratch)


_CMP_PAGES = 64
_CMP_BUFS = 2
_BLK_PER_PAGE = PAGE_SIZE // CMP_BLOCK
_TAIL_BLOCKS = 2 * SUBLANES


_PAIR_KEYS = 2 * PAGE_SIZE
_PAIR_BLOCKS = _PAIR_KEYS // CMP_BLOCK


def _compress_paged_kernel(pt_ref, cache_ref, tail_ref, perm_ref, w1_ref, b1_ref, w2_ref, kg_ref, o_ref,
                           buf_ref, xs_ref, sem_ref, *, n_seq, n_chunks, pages):
    def page_copy(page, slot, i):
        return pltpu.make_async_copy(
            cache_ref.at[page, pl.ds(0, _CMP_W), :],
            buf_ref.at[slot, :, pl.ds(i * PAGE_SIZE, PAGE_SIZE)], sem_ref.at[slot])

    nb = pages * _BLK_PER_PAGE

    def body(j, slot):
        for pp in range(pages // 2):
            cols = buf_ref[slot, :, pp * _PAIR_KEYS:(pp + 1) * _PAIR_KEYS]
            xp = _dot_t(perm_ref[...], cols.astype(BF16))
            xs_ref[pp] = xp.reshape(CMP_BLOCK, _PAIR_BLOCKS, _CMP_W)
        o_ref[pl.ds(pl.multiple_of(j * nb, nb), nb), :] = _compress_math(
            xs_ref, w1_ref, b1_ref, w2_ref, kg_ref, row_major=False)

    n_past = n_chunks * nb
    o_ref[n_past:n_past + _TAIL_BLOCKS, :] = _compress_math(tail_ref, w1_ref, b1_ref, w2_ref, kg_ref)
    _paged_loop(pt_ref, n_seq, n_chunks, pages, _CMP_BUFS, page_copy, body)


def _compress_paged(page_table, cache, tail, cw):
    n_seq, n_pages = page_table.shape
    pages = min(_CMP_PAGES, n_pages)
    assert pages % 2 == 0
    n_chunks = n_pages // pages
    n_out = n_pages * _BLK_PER_PAGE + _TAIL_BLOCKS
    key = np.arange(_PAIR_KEYS)
    perm = np.zeros((_PAIR_KEYS, _PAIR_KEYS), np.float32)
    perm[(key % CMP_BLOCK) * _PAIR_BLOCKS + key // CMP_BLOCK, key] = 1.0
    consts = [jnp.asarray(perm, BF16), cw['w1'], cw['b1'], cw['w2'], cw['kg']]
    cs = lambda c: pl.BlockSpec(c.shape, lambda s, pt: (0,) * c.ndim)
    kern = functools.partial(_compress_paged_kernel, n_seq=n_seq, n_chunks=n_chunks, pages=pages)
    return pl.pallas_call(
        kern,
        grid_spec=_seq_grid_spec(
            n_seq,
            [pl.BlockSpec(memory_space=pl.ANY),
             pl.BlockSpec((None, _TAIL_BLOCKS, CMP_BLOCK, _CMP_W), lambda s, pt: (s, 0, 0, 0))]
            + [cs(c) for c in consts],
            pl.BlockSpec((None, n_out, _CMP_W), lambda s, pt: (s, 0, 0)),
            [pltpu.VMEM((_CMP_BUFS, _CMP_W, pages * PAGE_SIZE), F32),
             pltpu.VMEM((pages // 2, CMP_BLOCK, _PAIR_BLOCKS, _CMP_W), F32),
             pltpu.SemaphoreType.DMA((_CMP_BUFS,))]),
        out_shape=jax.ShapeDtypeStruct((n_seq, n_out, _CMP_W), F32),
        compiler_params=_cparams(("arbitrary",)),
        name="compress_paged",
    )(page_table, cache, tail, *consts)


_SEL_PAGES = 32
_SEL_BUFS = 3


def _nsa_sample_kernel(pt_ref, rb_ref, q_ref, gate_ref, kc_ref, cache_ref, new_ref, wst_ref, wnew_ref,
                       o_ref, buf_ref, sem_ref, ms_ref, ls_ref, as_ref, mw_ref, lw_ref, aw_ref,
                       *, n_seq, n_chunks, pages, past_len, n_cmp_real, n_pick):
    tq = q_ref.shape[0]
    nrow = NSA_HEADS * tq
    kp = pages * PAGE_SIZE

    def page_copy(page, slot, i):
        return pltpu.make_async_copy(
            cache_ref.at[page, pl.ds(_WIN_W, _WIN_W), :],
            buf_ref.at[slot, :, pl.ds(i * PAGE_SIZE, PAGE_SIZE)], sem_ref.at[slot])

    q8 = _stack_heads(q_ref[...])

    def head_bias(dist):
        return jnp.concatenate([_bias_from_dist(dist, rb_ref, hd) for hd in range(NSA_HEADS)], axis=0)

    def rep_heads(x):
        return jnp.concatenate([x] * NSA_HEADS, axis=0)

    kc = kc_ref[...]
    n_cmp = kc.shape[0]
    n_sel = n_cmp // (SEL_BLOCK // CMP_BLOCK)
    q_pos_c = past_len + _row_iota((tq, n_cmp))
    blk_c = _lane_iota((tq, n_cmp))
    dist_c = q_pos_c - (blk_c * CMP_BLOCK + (CMP_BLOCK - 1))
    ok_c = rep_heads((dist_c >= 0) & (blk_c < n_cmp_real))
    s_c = jnp.where(ok_c, _dot_t(q8, kc[:, :LANES].astype(BF16)) + head_bias(dist_c), NEG_INF)
    e_c = jnp.exp(s_c - jnp.max(s_c, axis=-1, keepdims=True))
    p_c = jnp.where(ok_c, e_c / jnp.sum(e_c, axis=-1, keepdims=True), 0.0)
    o_c = _dot(p_c.astype(BF16), kc[:, LANES:].astype(BF16))

    imp_n = []
    for g in range(NSA_GROUPS):
        acc_g = p_c[(g * NSA_HPG) * tq:(g * NSA_HPG + 1) * tq]
        for hl in range(1, NSA_HPG):
            acc_g = acc_g + p_c[(g * NSA_HPG + hl) * tq:(g * NSA_HPG + hl + 1) * tq]
        imp_n.append(acc_g)
    imp_n = jnp.concatenate(imp_n, axis=0)
    pair = (_lane_iota((n_sel, n_cmp)) // (SEL_BLOCK // CMP_BLOCK) == _row_iota((n_sel, n_cmp)))
    pair = jnp.where(pair, 1.0, 0.0).astype(BF16)
    part_a = imp_n.astype(BF16)
    rest = imp_n - part_a.astype(F32)
    part_b = rest.astype(BF16)
    part_c = (rest - part_b.astype(F32)).astype(BF16)
    x = _dot_t(pair, part_a) + _dot_t(pair, part_b) + _dot_t(pair, part_c)
    blk = _row_iota(x.shape)
    cur = (past_len + _lane_iota(x.shape) % tq) // SEL_BLOCK
    forced = (blk == 0) | (blk == cur) | (blk == cur - 1)
    x = jnp.where(forced, FORCED_SCORE, jnp.where(blk <= cur, x, -1.0))
    picked_rows = []
    for _ in range(n_pick):
        m = jnp.max(x, axis=0, keepdims=True)
        idx = jnp.min(jnp.where(x == m, blk, n_sel), axis=0, keepdims=True)
        picked_rows.append(idx)
        x = jnp.where(blk == idx, -3e38, x)
    eye = _row_iota((NSA_GROUPS * tq, NSA_GROUPS * tq)) == _lane_iota((NSA_GROUPS * tq, NSA_GROUPS * tq))
    picked = [jnp.sum(jnp.where(eye, idx.astype(F32), 0.0), axis=1, keepdims=True) for idx in picked_rows]

    for m_ref, l_ref, a_ref in ((ms_ref, ls_ref, as_ref), (mw_ref, lw_ref, aw_ref)):
        m_ref[...] = jnp.full(m_ref.shape, NEG_INF, F32)
        l_ref[...] = jnp.zeros(l_ref.shape, F32)
        a_ref[...] = jnp.zeros(a_ref.shape, F32)

    def picked_mask(key_blk):
        key2 = jnp.concatenate([key_blk] * NSA_GROUPS, axis=0).astype(F32)
        hit = jnp.zeros(key2.shape, F32)
        for idx in picked:
            hit = jnp.where(key2 == idx, 1.0, hit)
        hit = hit > 0.5
        return jnp.concatenate([hit[:tq]] * NSA_HPG + [hit[tq:]] * NSA_HPG, axis=0)

    wst = wst_ref[...]
    w_s = wst.shape[1]
    dist_w = w_s + _row_iota((tq, w_s)) - _lane_iota((tq, w_s))
    _flash_step(_dot(q8, wst[:LANES].astype(BF16)) + head_bias(dist_w),
                rep_heads((dist_w >= 0) & (dist_w < WINDOW)), wst[LANES:].astype(BF16), mw_ref, lw_ref, aw_ref,
                v_feature_major=True)
    dist_n = _row_iota((tq, tq)) - _lane_iota((tq, tq))
    bias_n = head_bias(dist_n)
    ok_n = rep_heads(dist_n >= 0)
    wnew = wnew_ref[...]
    _flash_step(_dot_t(q8, wnew[:, :LANES].astype(BF16)) + bias_n, ok_n, wnew[:, LANES:].astype(BF16),
                mw_ref, lw_ref, aw_ref)

    new = new_ref[...]
    cur_new = (past_len + _lane_iota((tq, tq))) // SEL_BLOCK
    _flash_step(_dot_t(q8, new[:, :LANES].astype(BF16)) + bias_n, ok_n & picked_mask(cur_new),
                new[:, LANES:].astype(BF16), ms_ref, ls_ref, as_ref)

    far = head_bias(jnp.full((tq, PAGE_SIZE), REL_MAX_DIST, jnp.int32))
    near = head_bias(past_len + _row_iota((tq, PAGE_SIZE)) - (past_len - PAGE_SIZE + _lane_iota((tq, PAGE_SIZE))))

    def body(j, slot):
        cols = buf_ref[slot]
        last = jnp.where(j == n_chunks - 1, near, far)
        bias = jnp.concatenate([far] * (pages - 1) + [last], axis=1)
        key_blk = (j * kp + _lane_iota((tq, kp))) // SEL_BLOCK
        _flash_step(_dot(q8, cols[:LANES].astype(BF16)) + bias, picked_mask(key_blk),
                    cols[LANES:].astype(BF16), ms_ref, ls_ref, as_ref, v_feature_major=True)

    _paged_loop(pt_ref, n_seq, n_chunks, pages, _SEL_BUFS, page_copy, body)

    o_s = as_ref[...] / ls_ref[...]
    o_w = aw_ref[...] / lw_ref[...]
    gate = gate_ref[...]
    outs = []
    for hd in range(NSA_HEADS):
        sl = slice(hd * tq, (hd + 1) * tq)
        outs.append(gate[:, 3 * hd:3 * hd + 1] * o_c[sl] + gate[:, 3 * hd + 1:3 * hd + 2] * o_s[sl]
                    + gate[:, 3 * hd + 2:3 * hd + 3] * o_w[sl])
    o_ref[...] = _unstack_heads(jnp.concatenate(outs, axis=0), tq).astype(o_ref.dtype)


def _nsa_sample(page_table, rel_bias, qn, gate, kc, cache, rows_new, win_state, win_new, past_len):
    n_seq, n_pages = page_table.shape
    tq = qn.shape[1]
    pages = min(_SEL_PAGES, n_pages)
    n_chunks = n_pages // pages
    n_cmp = kc.shape[1]
    n_cmp_real = -(-(past_len + tq) // SEL_BLOCK) * SEL_BLOCK // CMP_BLOCK
    n_sel_real = n_cmp_real // (SEL_BLOCK // CMP_BLOCK)
    nrow = NSA_HEADS * tq
    kern = functools.partial(_nsa_sample_kernel, n_seq=n_seq, n_chunks=n_chunks, pages=pages, past_len=past_len,
                             n_cmp_real=n_cmp_real, n_pick=min(TOP_N, n_sel_real))
    per_seq = lambda shp, last=0: pl.BlockSpec((None,) + shp, lambda s, pt: (s, 0, last))
    stat = lambda w: pltpu.VMEM((nrow, w), F32)
    return pl.pallas_call(
        kern,
        grid_spec=_seq_grid_spec(
            n_seq,
            [pl.BlockSpec(memory_space=pltpu.SMEM), per_seq((tq, _QN_W)), per_seq((tq, LANES)),
             per_seq((n_cmp, _CMP_W)), pl.BlockSpec(memory_space=pl.ANY),
             per_seq((tq, _WIN_W), 1), per_seq((_WIN_W, win_state.shape[2])), per_seq((tq, _WIN_W))],
            per_seq((tq, _QN_W)),
            [pltpu.VMEM((_SEL_BUFS, _WIN_W, pages * PAGE_SIZE), F32), pltpu.SemaphoreType.DMA((_SEL_BUFS,)),
             stat(1), stat(1), stat(LANES), stat(1), stat(1), stat(LANES)]),
        out_shape=jax.ShapeDtypeStruct((n_seq, tq, _QN_W), BF16),
        compiler_params=_cparams(("arbitrary",)),
        name="nsa_sample",
    )(page_table, rel_bias, qn, gate, kc, cache, rows_new, win_state, win_new)


_MLA_PAGES = 32
_MLA_BUFS = 2


def _mla_sample_kernel(pt_ref, q_ref, cache_ref, new_ref, tab_ref, tabn_ref, wknt_ref, wv_ref, gk_ref,
                       o_ref, buf_ref, sem_ref, m_ref, l_ref, acc_ref, *, n_seq, n_chunks, pages):
    tq = o_ref.shape[0]
    nrow = MLA_HEADS * tq
    kp = pages * PAGE_SIZE

    def page_copy(page, slot, i):
        return pltpu.make_async_copy(
            cache_ref.at[page], buf_ref.at[slot, :, pl.ds(i * PAGE_SIZE, PAGE_SIZE)], sem_ref.at[slot])

    q = q_ref[...].astype(F32)
    lane = _lane_iota(q.shape)
    gk = gk_ref[...]
    qg = q * gk
    nope = jnp.where(lane < MLA_NOPE_DIM, qg, 0.0)
    nope = nope + pltpu.roll(nope, MLA_NOPE_DIM, 1)
    qn = jnp.concatenate([nope] * (MLA_HEADS * MLA_NOPE_DIM // LANES), axis=1)
    own = _lane_iota(qn.shape) // MLA_NOPE_DIM == _row_iota(qn.shape) // tq
    q_abs = _dot(jnp.where(own, qn, 0.0).astype(BF16), wknt_ref[...])
    up =pltpu.roll(q, LANES - ROPE_HALF, 1)
    dn = pltpu.roll(q, ROPE_HALF, 1)
    in_x1 = (lane >= MLA_NOPE_DIM) & (lane < MLA_NOPE_DIM + ROPE_HALF)
    in_x2 = (lane >= MLA_NOPE_DIM + ROPE_HALF) & (lane < MLA_QK_DIM)
    q_hat = jnp.where(in_x1, up, jnp.where(in_x2, -dn, 0.0)) * gk
    q_cos = pltpu.roll(jnp.where(lane >= MLA_NOPE_DIM, qg, 0.0), LANES - MLA_NOPE_DIM, 1)
    q_sin = pltpu.roll(q_hat, LANES - MLA_ROPE_DIM, 1)
    ones_rows = jnp.where((_lane_iota((SUBLANES, LANES)) >= 2 * MLA_ROPE_DIM), 1.0, 0.0)
    lhs_r = jnp.concatenate([jnp.where(lane < MLA_ROPE_DIM, q_cos, jnp.where(lane < 2 * MLA_ROPE_DIM, q_sin, 0.0)),
                             ones_rows], axis=0).astype(BF16)
    lhs_c = jnp.concatenate([wknt_ref[...], q_abs.astype(BF16)], axis=0)
    n_kn = MLA_HEADS * MLA_NOPE_DIM

    m_ref[...] = jnp.full(m_ref.shape, NEG_INF, F32)
    l_ref[...] = jnp.zeros(l_ref.shape, F32)
    acc_ref[...] = jnp.zeros(acc_ref.shape, F32)

    def attend(cols, tab, allowed):
        nk = cols.shape[1]
        c = cols[:MLA_KV_RANK].astype(BF16)
        kr = cols[MLA_KV_RANK:]
        sq = kr * kr
        hi = sq.astype(BF16).astype(F32)
        rhs = (jnp.concatenate([kr, kr, hi, sq - hi], axis=0) * tab).astype(BF16)
        sr = _dot(lhs_r, rhs)
        both = _dot(lhs_c, c)
        inv = []
        for hd in range(MLA_HEADS):
            blk = both[hd * MLA_NOPE_DIM:(hd + 1) * MLA_NOPE_DIM]
            ms = (jnp.sum(blk * blk, axis=0, keepdims=True) + sr[nrow:nrow + 1]) * (1.0 / MLA_QK_DIM)
            inv.append(jnp.broadcast_to(lax.rsqrt(ms + NORM_EPS), (tq, nk)))
        s = (both[n_kn:] + sr[:nrow]) * jnp.concatenate(inv, axis=0)
        if allowed is not None:
            s = jnp.where(allowed, s, NEG_INF)
        m_old = m_ref[...]
        m_new = jnp.maximum(m_old, jnp.max(s, axis=-1, keepdims=True))
        alpha = jnp.exp(m_old - m_new)
        p = jnp.exp(s - m_new)
        l_ref[...] = alpha * l_ref[...] + jnp.sum(p, axis=-1, keepdims=True)
        acc_ref[...] = alpha * acc_ref[...] + _dot_t(p.astype(BF16), c)
        m_ref[...] = m_new

    causal = _row_iota((tq, tq)) >= _lane_iota((tq, tq))
    attend(new_ref[...], tabn_ref[...], jnp.concatenate([causal] * MLA_HEADS, axis=0))

    def body(j, slot):
        attend(buf_ref[slot], tab_ref[:, pl.ds(pl.multiple_of(j * kp, kp), kp)], None)

    _paged_loop(pt_ref, n_seq, n_chunks, pages, _MLA_BUFS, page_copy, body)

    o = acc_ref[...] / l_ref[...]
    outs = [_dot(o[hd * tq:(hd + 1) * tq].astype(BF16), wv_ref[hd]) for hd in range(MLA_HEADS)]
    o_ref[...] = jnp.concatenate(outs, axis=1).astype(o_ref.dtype)


def _mla_key_tables(pos):
    inv = jnp.power(ROPE_BASE, -jnp.arange(ROPE_HALF, dtype=F32) / ROPE_HALF)
    ang = pos.astype(F32)[:, None] * inv[None, :]
    cos, sin = jnp.cos(ang).T, jnp.sin(ang).T
    return jnp.concatenate([cos, cos, sin, sin, jnp.ones((LANES - 2 * MLA_ROPE_DIM, pos.shape[0]), F32)], axis=0)


def _mla_sample(page_table, q_rows, cache, rows_new, tab_past, tab_new, wknt, wv, gk):
    n_seq, n_pages = page_table.shape
    tq = rows_new.shape[2]
    nrow = MLA_HEADS * tq
    pages = min(_MLA_PAGES, n_pages)
    n_chunks = n_pages // pages
    kern = functools.partial(_mla_sample_kernel, n_seq=n_seq, n_chunks=n_chunks, pages=pages)
    per_seq = lambda shp: pl.BlockSpec((None,) + shp, lambda s, pt: (s, 0, 0))
    cs = lambda c: pl.BlockSpec(c.shape, lambda s, pt: (0,) * c.ndim)
    ow = MLA_HEADS * MLA_V_DIM
    return pl.pallas_call(
        kern,
        grid_spec=_seq_grid_spec(
            n_seq,
            [per_seq((nrow, LANES)), pl.BlockSpec(memory_space=pl.ANY), per_seq((_MLA_ROW_W, tq)),
             cs(tab_past), cs(tab_new), cs(wknt), cs(wv), cs(gk)],
            per_seq((tq, ow)),
            [pltpu.VMEM((_MLA_BUFS, _MLA_ROW_W, pages * PAGE_SIZE), F32), pltpu.SemaphoreType.DMA((_MLA_BUFS,)),
             pltpu.VMEM((nrow, 1), F32), pltpu.VMEM((nrow, 1), F32), pltpu.VMEM((nrow, MLA_KV_RANK), F32)]),
        out_shape=jax.ShapeDtypeStruct((n_seq, tq, ow), BF16),
        compiler_params=_cparams(("arbitrary",)),
        name="mla_sample",
    )(page_table, q_rows, cache, rows_new, tab_past, tab_new, wknt, wv, gk)


_MEM_SEQS_PER_STEP = 8


_TOKEN_TILE = 512


def _token_tile(n):
    return min(_TOKEN_TILE, n)


def _after_mixers(x, o_n, o_m, mem_kv, mem_tokens, mem_seqs, lw, past, seq_len, tm):
    x1, qmem = _merge(x, o_n, o_m, lw['g1'], lw['wmg'], lw['wno'], lw['wmo'], lw['wout'],
                      lw['g2'], lw['mem_wq'], lw['mem_qg'], tm)
    om = _mem_attn(qmem, mem_kv, mem_tokens, mem_seqs)
    return _ffn(x1, om, lw['mem_wo'], lw['g3'], lw['wup'], lw['conv_w'], lw['conv_b'], lw['wdn'],
                past, tm, seq_len)


def kernel(x_prompt, x_sample, cache_nsa, cache_mla, state_win, cache_mem, state_conv, page_table,
           mem_prompt, rel_bias, norm1_g, w_in, nsa_q_norm, nsa_k_norm, nsa_cmp_w1, nsa_cmp_b1,
           nsa_cmp_w2, nsa_w_o, mla_q_norm, mla_kv_norm, mla_w_uq, mla_w_ukv, mla_qk_q_norm,
           mla_qk_k_norm, mla_w_o, w_out, norm2_g, mem_norm_g, mem_w_q, mem_w_kv, mem_q_norm,
           mem_k_norm, mem_w_o, norm3_g, ffn_w_up, ffn_conv_w, ffn_conv_b, ffn_w_down):
    depth = w_in.shape[0]
    bsz, t, _ = x_prompt.shape
    n_seq, tq, _ = x_sample.shape
    n_pool = cache_nsa.shape[1]
    n_pages = page_table.shape[1]
    past_len = n_pages * PAGE_SIZE
    w_s = state_win.shape[2]
    mem_len = mem_prompt.shape[1]
    assert t % _MLA_TK == 0 and _MLA_TK % _TM == 0 and past_len % SEL_BLOCK == 0 and tq <= SEL_BLOCK and tq % SUBLANES == 0
    layer = lambda a, l: a.reshape(a.shape[1:]) if depth == 1 else a[l]

    xp = x_prompt.reshape(bsz * t, D_MODEL)
    xs = x_sample.reshape(n_seq * tq, D_MODEL)
    tm_p = _token_tile(t)
    tm_s = _token_tile(n_seq * tq)
    assert t % tm_p == 0 and (n_seq * tq) % tm_s == 0 and tm_s % tq == 0
    tabs_p = _rope_tables(jnp.arange(t, dtype=jnp.int32))
    pos_s = past_len + jnp.arange(tq, dtype=jnp.int32)
    proj_p, proj_s = min(_TM, t), min(_TM, n_seq * tq)
    assert t % proj_p == 0 and (n_seq * tq) % proj_s == 0 and proj_s % tq == 0
    tabs_s = tuple(jnp.tile(tb, (proj_s // tq, 1)) for tb in _rope_tables(pos_s))
    key_tab_past = _mla_key_tables(jnp.arange(past_len, dtype=jnp.int32))
    key_tab_new = _mla_key_tables(pos_s)
    cache_nsa_fm = jnp.transpose(cache_nsa, (0, 1, 3, 4, 5, 2)).reshape(depth * n_pool, _ROWS_W, PAGE_SIZE)
    cache_mla_fm = jnp.transpose(cache_mla, (0, 1, 3, 2)).reshape(depth * n_pool, _MLA_ROW_W, PAGE_SIZE)
    mem_flat = mem_prompt.reshape(bsz * mem_len, D_MODEL)

    outs = [[] for _ in range(9)]
    for l in range(depth):
        pw = _proj_weights(w_in[l], nsa_q_norm[l], nsa_k_norm[l], mla_q_norm[l], mla_kv_norm[l],
                           mla_w_uq[l], mla_qk_q_norm[l], mla_w_ukv[l], mla_qk_k_norm[l])
        cw = _compress_weights(nsa_cmp_w1[l], nsa_cmp_b1[l], nsa_cmp_w2[l], nsa_k_norm[l, 0])
        wno = nsa_w_o[l].reshape(NSA_GROUPS, NSA_HPG, NSA_HEAD_DIM, D_MODEL)
        wno = jnp.transpose(wno, (1, 0, 2, 3)).reshape(_QN_W, D_MODEL)
        lw = dict(g1=norm1_g[l], wmg=w_in[l][:, -2 * D_MODEL:].astype(BF16), wno=wno.astype(BF16),
                  wmo=mla_w_o[l].astype(BF16), wout=w_out[l].astype(BF16), g2=norm2_g[l],
                  mem_wq=mem_w_q[l].astype(BF16), mem_qg=mem_q_norm[l], mem_wo=mem_w_o[l].astype(BF16),
                  g3=norm3_g[l], wup=ffn_w_up[l].astype(BF16), conv_w=ffn_conv_w[l], conv_b=ffn_conv_b[l],
                  wdn=ffn_w_down[l].astype(BF16))

        pr = _project(xp, tabs_p, t // proj_p, norm1_g[l], pw, proj_p)
        nb_p = bsz * t // CMP_BLOCK
        kc_p = _compress_rows(pr['rows'].reshape(nb_p, CMP_BLOCK, _ROWS_W), cw, min(LANES, nb_p))
        per_b = lambda a: a.reshape(bsz, t, a.shape[-1])
        o_n = _nsa_prompt(per_b(pr['qn']), per_b(pr['gate']), kc_p.reshape(bsz, t // CMP_BLOCK, _CMP_W),
                          per_b(pr['rows']), per_b(pr['win']), rel_bias)
        o_m = _mla_prompt(per_b(pr['qm']), per_b(pr['kmla']), per_b(pr['cb']), pw['wv'])
        mkv = _mem_kv(mem_flat, mem_norm_g[l], mem_w_kv[l].astype(BF16), mem_k_norm[l])
        xp, a_tail = _after_mixers(xp, o_n.reshape(bsz * t, -1), o_m.reshape(bsz * t, -1),
                                   mkv.reshape(bsz, mem_len, 2 * _MEM_W), _TM, 1, lw, None, t, tm_p)
        outs[0].append(pr['rows'].reshape(bsz, t, 4, NSA_GROUPS, NSA_HEAD_DIM))
        outs[1].append(pr['mrow'].reshape(bsz, t, _MLA_ROW_W))
        outs[2].append(pr['win'].reshape(bsz, t, 2, NSA_GROUPS, NSA_HEAD_DIM)[:, -min(WINDOW, t):])
        outs[3].append(mkv.reshape(bsz, mem_len, 2, MEM_HEADS, MEM_HEAD_DIM))
        outs[4].append(a_tail.reshape(bsz, t // tm_p, _CARRY_ROWS, D_FF)[:, -1, _CARRY_ROWS - (CONV_W - 1):])

        sr = _project(xs, tabs_s, 1, norm1_g[l], pw, proj_s)
        rows_s = sr['rows'].reshape(n_seq, tq, _ROWS_W)
        win_new = sr['win'].reshape(n_seq, tq, _WIN_W)
        pt_l = page_table + l * n_pool
        tail = jnp.pad(rows_s, ((0, 0), (0, _TAIL_BLOCKS * CMP_BLOCK - tq), (0, 0)))
        kc_s = _compress_paged(pt_l, cache_nsa_fm, tail.reshape(n_seq, _TAIL_BLOCKS, CMP_BLOCK, _ROWS_W), cw)
        win_old = jnp.transpose(layer(state_win, l), (0, 2, 3, 4, 1)).reshape(n_seq, _WIN_W, w_s)
        o_n = _nsa_sample(pt_l, rel_bias, sr['qn'].reshape(n_seq, tq, _QN_W), sr['gate'].reshape(n_seq, tq, LANES),
                          kc_s, cache_nsa_fm, rows_s, win_old, win_new, past_len)
        q_rows = jnp.transpose(sr['qm'].reshape(n_seq, tq, MLA_HEADS, MLA_HEAD_PAD), (0, 2, 1, 3))
        mrow_s = sr['mrow'].reshape(n_seq, tq, _MLA_ROW_W)
        o_m = _mla_sample(pt_l, q_rows.reshape(n_seq, MLA_HEADS * tq, MLA_HEAD_PAD), cache_mla_fm,
                          jnp.transpose(mrow_s, (0, 2, 1)), key_tab_past, key_tab_new,
                          pw['wkn_flat'].T, pw['wv'], pw['qkk'])
        past = jnp.pad(layer(state_conv, l), ((0, 0), (0, tq - (CONV_W - 1)), (0, 0)))
        xs, a_s = _after_mixers(xs, o_n.reshape(n_seq * tq, -1), o_m.reshape(n_seq * tq, -1),
                                layer(cache_mem, l).reshape(n_seq, mem_len, 2 * _MEM_W), tq,
                                min(_MEM_SEQS_PER_STEP, n_seq), lw, past.reshape(n_seq * tq, D_FF), tq, tm_s)
        outs[5].append(rows_s.reshape(n_seq, tq, 4, NSA_GROUPS, NSA_HEAD_DIM))
        outs[6].append(mrow_s)
        win_all = jnp.concatenate([win_old, jnp.transpose(win_new, (0, 2, 1))], axis=2)[:, :, -w_s:]
        win_all = win_all.reshape(n_seq, 2, NSA_GROUPS, NSA_HEAD_DIM, w_s)
        outs[7].append(jnp.transpose(win_all, (0, 4, 1, 2, 3)))
        outs[8].append(a_s.reshape(n_seq, tq, D_FF)[:, tq - (CONV_W - 1):])

    return (xp.reshape(bsz, t, D_MODEL), xs.reshape(n_seq, tq, D_MODEL)) + tuple(jnp.stack(o) for o in outs)
```
